```python
import jax, jax.numpy as jnp
from jax import lax
import numpy as np

D_MODEL = 4096
BATCH = 1
SEQ = 16384
DEPTH = 4

N_MIXERS = 3
LAYER_MIXERS = tuple("ABC"[i % N_MIXERS] for i in range(DEPTH))

GMLP_D = D_MODEL
GMLP_CHUNK = 128
GMLP_GROUPS = 8
GMLP_GROUP_DIM = GMLP_D // GMLP_GROUPS

MOBA_HEADS = 32
MOBA_HEAD_DIM = D_MODEL // MOBA_HEADS
MOBA_BLOCK = 256
MOBA_TOPK = 3
MOBA_Q_CHUNK = 16
ROPE_THETA = 500000.0
ROPE_DIM = MOBA_HEAD_DIM // 4

MLSTM_HEADS = 8
MLSTM_QK_DIM = D_MODEL // MLSTM_HEADS // 2
MLSTM_V_DIM = D_MODEL // MLSTM_HEADS
MLSTM_CHUNK = 64

D_FF = 11008
CONV_W = 3

NEG_INF = -1e30

kernel_name = "hybrid_gmlp_moba_mlstm_convffn"


def rmsnorm(x, g, eps=1e-6):
    xf = x.astype(jnp.float32)
    y = xf * lax.rsqrt(jnp.mean(xf * xf, axis=-1, keepdims=True) + eps)
    return y.astype(x.dtype) * g


def layernorm(x, g, b, eps=1e-5):
    xf = x.astype(jnp.float32)
    mu = jnp.mean(xf, axis=-1, keepdims=True)
    var = jnp.mean(jnp.square(xf - mu), axis=-1, keepdims=True)
    return ((xf - mu) * lax.rsqrt(var + eps)).astype(x.dtype) * g + b


def partial_rotary(x, pos):
    inv = ROPE_THETA ** (-jnp.arange(0, ROPE_DIM, 2, dtype=jnp.float32) / ROPE_DIM)
    ang = pos.astype(jnp.float32)[:, None] * inv[None, :]
    cos = jnp.cos(ang)[None, :, None, :]
    sin = jnp.sin(ang)[None, :, None, :]
    xr = x[..., :ROPE_DIM].astype(jnp.float32)
    x1, x2 = xr[..., :ROPE_DIM // 2], xr[..., ROPE_DIM // 2:]
    rot = jnp.concatenate([x1 * cos - x2 * sin, x2 * cos + x1 * sin], axis=-1).astype(x.dtype)
    return jnp.concatenate([rot, x[..., ROPE_DIM:]], axis=-1)


def gmlp_mixer(h, w_in, ln_g, ln_b, w_s, b_s, w_out):
    B, S, _ = h.shape
    z = jax.nn.gelu(h @ w_in, approximate=False)
    u, v = jnp.split(z, 2, axis=-1)
    v = layernorm(v, ln_g, ln_b)
    nc = S // GMLP_CHUNK
    v = v.reshape(B, nc, GMLP_CHUNK, GMLP_GROUPS, GMLP_GROUP_DIM)
    w = w_s * jnp.tril(jnp.ones((GMLP_CHUNK, GMLP_CHUNK), w_s.dtype))
    sv = jnp.einsum("gts,bcsgd->bctgd", w, v) + b_s.T[None, None, :, :, None]
    y = u * sv.reshape(B, S, GMLP_D)
    return y @ w_out


def moba_mixer(h, w_qkv, w_out):
    B, S, _ = h.shape
    H, DH, BLK, QC = MOBA_HEADS, MOBA_HEAD_DIM, MOBA_BLOCK, MOBA_Q_CHUNK
    pos = jnp.arange(S)
    q, k, v = jnp.split(h @ w_qkv, 3, axis=-1)
    q = partial_rotary(q.reshape(B, S, H, DH), pos)
    k = partial_rotary(k.reshape(B, S, H, DH), pos)
    v = v.reshape(B, S, H, DH)
    nb = -(-S // BLK)
    pad = nb * BLK - S
    q = q.transpose(0, 2, 1, 3)

    def to_blocks(t):
        t = jnp.pad(t, ((0, 0), (0, pad), (0, 0), (0, 0)))
        return t.reshape(B, nb, BLK, H, DH).transpose(0, 3, 1, 2, 4)

    kb, vb = to_blocks(k), to_blocks(v)
    kmean = jnp.mean(kb.astype(jnp.float32), axis=3)
    n_sel = min(MOBA_TOPK, nb)
    bi = jnp.arange(B)[:, None, None, None]
    hi = jnp.arange(H)[None, :, None, None]
    scale = DH ** -0.5

    def attend_chunk(c):
        start = c * QC
        qc = lax.dynamic_slice_in_dim(q, start, QC, axis=2)
        own = start // BLK
        qpos = start + jnp.arange(QC)
        gate = jnp.einsum("bhqd,bhnd->bhqn", qc.astype(jnp.float32), kmean)
        gate = jnp.where(jnp.arange(nb) < own, gate, NEG_INF)
        _, idx = lax.top_k(gate, n_sel)
        kg = kb[bi, hi, idx]
        vg = vb[bi, hi, idx]
        s_sel = jnp.einsum("bhqd,bhqkld->bhqkl", qc, kg).astype(jnp.float32) * scale
        s_sel = jnp.where((jnp.arange(n_sel) < own)[:, None], s_sel, NEG_INF)
        ko = lax.dynamic_index_in_dim(kb, own, axis=2, keepdims=False)
        vo = lax.dynamic_index_in_dim(vb, own, axis=2, keepdims=False)
        s_own = jnp.einsum("bhqd,bhld->bhql", qc, ko).astype(jnp.float32) * scale
        kpos = own * BLK + jnp.arange(BLK)
        s_own = jnp.where(kpos[None, :] <= qpos[:, None], s_own, NEG_INF)
        s = jnp.concatenate([s_sel.reshape(B, H, QC, n_sel * BLK), s_own], axis=-1)
        p = jax.nn.softmax(s, axis=-1).astype(v.dtype)
        p_sel = p[..., :n_sel * BLK].reshape(B, H, QC, n_sel, BLK)
        p_own = p[..., n_sel * BLK:]
        return (jnp.einsum("bhqkl,bhqkld->bhqd", p_sel, vg)
                + jnp.einsum("bhql,bhld->bhqd", p_own, vo))

    out = lax.map(attend_chunk, jnp.arange(S // QC))
    out = out.transpose(1, 0, 3, 2, 4).reshape(B, S, H * DH)
    return out @ w_out


def mlstm_mixer(h, w_in, b_i, b_f, w_out):
    B, S, _ = h.shape
    H, DK, DV, L = MLSTM_HEADS, MLSTM_QK_DIM, MLSTM_V_DIM, MLSTM_CHUNK
    nc = S // L
    splits = [H * DK, 2 * H * DK, 2 * H * DK + H * DV, 2 * H * DK + 2 * H * DV, 2 * H * DK + 2 * H * DV + H]
    q, k, v, og, ig, fg = jnp.split(h @ w_in, splits, axis=-1)

    def heads(t, d):
        return t.astype(jnp.float32).reshape(B, nc, L, H, d).transpose(1, 0, 3, 2, 4)

    def gates(t):
        return t.reshape(B, nc, L, H).transpose(1, 0, 3, 2)

    qh = heads(q, DK) * (DK ** -0.5)
    kh = heads(k, DK)
    vh = heads(v, DV)
    li = gates((ig + b_i).astype(jnp.float32))
    lf = gates(jax.nn.log_sigmoid((fg + b_f).astype(jnp.float32)))
    causal = jnp.tril(jnp.ones((L, L), dtype=bool))

    def chunk_step(carry, xs):
        C, n, m = carry
        qc, kc, vc, lic, lfc = xs
        b = jnp.cumsum(lfc, axis=-1)
        dmat = jnp.where(causal, b[..., :, None] - b[..., None, :] + lic[..., None, :], NEG_INF)
        inter = b + m[..., None]
        m_t = jnp.maximum(inter, jnp.max(dmat, axis=-1))
        a = jnp.exp(dmat - m_t[..., None]) * jnp.einsum("bhtd,bhsd->bhts", qc, kc)
        s_inter = jnp.exp(inter - m_t)
        num = s_inter[..., None] * jnp.einsum("bhtd,bhde->bhte", qc, C) + jnp.einsum("bhts,bhse->bhte", a, vc)
        den = s_inter * jnp.einsum("bhtd,bhd->bht", qc, n) + jnp.sum(a, axis=-1)
        h_t = num / jnp.maximum(jnp.abs(den), jnp.exp(-m_t))[..., None]
        b_last = b[..., -1]
        g = b_last[..., None] - b + lic
        m_new = jnp.maximum(b_last + m, jnp.max(g, axis=-1))
        decay = jnp.exp(b_last + m - m_new)
        wk = jnp.exp(g - m_new[..., None])
        C_new = decay[..., None, None] * C + jnp.einsum("bhs,bhsd,bhse->bhde", wk, kc, vc)
        n_new = decay[..., None] * n + jnp.einsum("bhs,bhsd->bhd", wk, kc)
        return (C_new, n_new, m_new), h_t

    init = (jnp.zeros((B, H, DK, DV), jnp.float32), jnp.zeros((B, H, DK), jnp.float32), jnp.zeros((B, H), jnp.float32))
    _, hs = lax.scan(chunk_step, init, (qh, kh, vh, li, lf))
    hs = hs.transpose(1, 0, 3, 2, 4).reshape(B, S, H * DV).astype(h.dtype)
    return (jax.nn.sigmoid(og) * hs) @ w_out


def conv_ffn(h, w_up, conv_w, conv_b, w_down):
    S = h.shape[1]
    a = h @ w_up
    ap = jnp.pad(a, ((0, 0), (CONV_W - 1, 0), (0, 0)))
    c = conv_b + conv_w[0] * ap[:, 0:S]
    for j in range(1, CONV_W):
        c = c + conv_w[j] * ap[:, j:j + S]
    gate, up = jnp.split(c, 2, axis=-1)
    return (jax.nn.gelu(gate, approximate=False) * up) @ w_down


def setup_inputs(seed: int = 0) -> dict:
    key = jax.random.key(seed)
    ks = iter(jax.random.split(key, 128))

    def dense(fi, fo):
        return jax.random.normal(next(ks), (fi, fo), jnp.float32) * fi ** -0.5

    def gain(n):
        return 1.0 + 0.05 * jax.random.normal(next(ks), (n,), jnp.float32)

    def small(shape, s=0.02):
        return s * jax.random.normal(next(ks), shape, jnp.float32)

    inp = {"x": jax.random.normal(next(ks), (BATCH, SEQ, D_MODEL), jnp.float32)}
    for i in range(DEPTH):
        p = f"l{i}_"
        kind = LAYER_MIXERS[i]
        inp[p + "norm_mix"] = gain(D_MODEL)
        if kind == "A":
            inp[p + "a_w_in"] = dense(D_MODEL, 2 * GMLP_D)
            inp[p + "a_ln_g"] = gain(GMLP_D)
            inp[p + "a_ln_b"] = small((GMLP_D,))
            inp[p + "a_w_s"] = jax.random.normal(next(ks), (GMLP_GROUPS, GMLP_CHUNK, GMLP_CHUNK), jnp.float32) * GMLP_CHUNK ** -0.5
            inp[p + "a_b_s"] = 1.0 + 0.1 * jax.random.normal(next(ks), (GMLP_GROUPS, GMLP_CHUNK), jnp.float32)
            inp[p + "a_w_out"] = dense(GMLP_D, D_MODEL)
        elif kind == "B":
            inp[p + "b_w_qkv"] = dense(D_MODEL, 3 * MOBA_HEADS * MOBA_HEAD_DIM)
            inp[p + "b_w_out"] = dense(MOBA_HEADS * MOBA_HEAD_DIM, D_MODEL)
        else:
            n_in = 2 * MLSTM_HEADS * MLSTM_QK_DIM + 2 * MLSTM_HEADS * MLSTM_V_DIM + 2 * MLSTM_HEADS
            inp[p + "c_w_in"] = dense(D_MODEL, n_in)
            inp[p + "c_b_i"] = small((MLSTM_HEADS,), 0.1)
            inp[p + "c_b_f"] = 3.0 + 0.1 * jax.random.normal(next(ks), (MLSTM_HEADS,), jnp.float32)
            inp[p + "c_w_out"] = dense(MLSTM_HEADS * MLSTM_V_DIM, D_MODEL)
        inp[p + "norm_ffn"] = gain(D_MODEL)
        inp[p + "ffn_w_up"] = dense(D_MODEL, 2 * D_FF)
        inp[p + "ffn_conv_w"] = jax.random.normal(next(ks), (CONV_W, 2 * D_FF), jnp.float32) * CONV_W ** -0.5
        inp[p + "ffn_conv_b"] = small((2 * D_FF,))
        inp[p + "ffn_w_down"] = dense(D_FF, D_MODEL)
    inp["final_norm"] = gain(D_MODEL)
    return inp


def reference(x,
              l0_norm_mix, l0_a_w_in, l0_a_ln_g, l0_a_ln_b, l0_a_w_s, l0_a_b_s, l0_a_w_out,
              l0_norm_ffn, l0_ffn_w_up, l0_ffn_conv_w, l0_ffn_conv_b, l0_ffn_w_down,
              l1_norm_mix, l1_b_w_qkv, l1_b_w_out,
              l1_norm_ffn, l1_ffn_w_up, l1_ffn_conv_w, l1_ffn_conv_b, l1_ffn_w_down,
              l2_norm_mix, l2_c_w_in, l2_c_b_i, l2_c_b_f, l2_c_w_out,
              l2_norm_ffn, l2_ffn_w_up, l2_ffn_conv_w, l2_ffn_conv_b, l2_ffn_w_down,
              l3_norm_mix, l3_a_w_in, l3_a_ln_g, l3_a_ln_b, l3_a_w_s, l3_a_b_s, l3_a_w_out,
              l3_norm_ffn, l3_ffn_w_up, l3_ffn_conv_w, l3_ffn_conv_b, l3_ffn_w_down,
              final_norm):
    layers = (
        (l0_norm_mix, (l0_a_w_in, l0_a_ln_g, l0_a_ln_b, l0_a_w_s, l0_a_b_s, l0_a_w_out),
         l0_norm_ffn, (l0_ffn_w_up, l0_ffn_conv_w, l0_ffn_conv_b, l0_ffn_w_down)),
        (l1_norm_mix, (l1_b_w_qkv, l1_b_w_out),
         l1_norm_ffn, (l1_ffn_w_up, l1_ffn_conv_w, l1_ffn_conv_b, l1_ffn_w_down)),
        (l2_norm_mix, (l2_c_w_in, l2_c_b_i, l2_c_b_f, l2_c_w_out),
         l2_norm_ffn, (l2_ffn_w_up, l2_ffn_conv_w, l2_ffn_conv_b, l2_ffn_w_down)),
        (l3_norm_mix, (l3_a_w_in, l3_a_ln_g, l3_a_ln_b, l3_a_w_s, l3_a_b_s, l3_a_w_out),
         l3_norm_ffn, (l3_ffn_w_up, l3_ffn_conv_w, l3_ffn_conv_b, l3_ffn_w_down)),
    )
    mixers = {"A": gmlp_mixer, "B": moba_mixer, "C": mlstm_mixer}
    for i in range(DEPTH):
        norm_mix, mix_params, norm_ffn, ffn_params = layers[i]
        x = x + mixers[LAYER_MIXERS[i]](rmsnorm(x, norm_mix), *mix_params)
        x = x + conv_ffn(rmsnorm(x, norm_ffn), *ffn_params)
    return rmsnorm(x, final_norm)
```

```python
import functools
import math

import jax
import jax.numpy as jnp
from jax import lax
from jax.experimental import pallas as pl
from jax.experimental.pallas import tpu as pltpu

F32 = jnp.float32
BF16 = jnp.bfloat16

V7X_LANES = 128
V7X_SUBLANES = 8
V7X_VMEM_BYTES = 64 * 1024 * 1024
VMEM_LIMIT_BYTES = V7X_VMEM_BYTES - 8 * 1024 * 1024

GMLP_CHUNK = 128
GMLP_GROUPS = 8
MOBA_HEADS = 32
MOBA_BLOCK = 256
MOBA_TOPK = 3
ROPE_THETA = 500000.0
MLSTM_HEADS = 8
CONV_W = 3
NEG_INF = -1e30
RMS_EPS = 1e-6
LN_EPS = 1e-5
LOG2E = math.log2(math.e)


def _params(semantics):
    return pltpu.CompilerParams(dimension_semantics=semantics, vmem_limit_bytes=VMEM_LIMIT_BYTES)


def _rmsnorm_kernel(x_ref, g_ref, o_ref):
    x = x_ref[...]
    y = x * lax.rsqrt(jnp.mean(x * x, axis=-1, keepdims=True) + RMS_EPS)
    o_ref[...] = (y * g_ref[...]).astype(o_ref.dtype)


def _rmsnorm(x, g, out_dtype, rows=256):
    s, d = x.shape
    return pl.pallas_call(
        _rmsnorm_kernel,
        grid=(s // rows,),
        in_specs=[pl.BlockSpec((rows, d), lambda i: (i, 0)), pl.BlockSpec((1, d), lambda i: (0, 0))],
        out_specs=pl.BlockSpec((rows, d), lambda i: (i, 0)),
        out_shape=jax.ShapeDtypeStruct((s, d), out_dtype),
        compiler_params=_params(("parallel",)),
        name="rmsnorm",
    )(x, g.reshape(1, d))


def _gelu(x):
    return 0.5 * x * (1.0 + lax.erf(x * math.sqrt(0.5)))


def _rope_heads(acc, cos, sin):
    rows, width = acc.shape
    lane = lax.broadcasted_iota(jnp.int32, (rows, V7X_LANES), 1)
    half = cos.shape[-1] // 8
    outs = []
    for h in range(width // V7X_LANES):
        xh = acc[:, h * V7X_LANES:(h + 1) * V7X_LANES]
        partner = jnp.where(lane < half, pltpu.roll(xh, V7X_LANES - half, axis=1), pltpu.roll(xh, half, axis=1))
        outs.append(xh * cos + partner * sin)
    return jnp.concatenate(outs, axis=1) if len(outs) > 1 else outs[0]


def _mm_kernel(*refs, epilogue, rope_tiles):
    x_ref, w_ref = refs[0], refs[1]
    o_ref = refs[-1]
    acc = jnp.dot(x_ref[...], w_ref[...], preferred_element_type=F32)
    if epilogue == "cast":
        o_ref[...] = acc.astype(o_ref.dtype)
    elif epilogue == "gelu":
        o_ref[...] = _gelu(acc).astype(o_ref.dtype)
    elif epilogue == "residual":
        o_ref[...] = (refs[2][...] + acc).astype(o_ref.dtype)
    elif epilogue == "rope":
        cos_ref, sin_ref = refs[2], refs[3]
        j = pl.program_id(1)

        @pl.when(j < rope_tiles)
        def _():
            o_ref[...] = _rope_heads(acc, cos_ref[...], sin_ref[...]).astype(o_ref.dtype)

        @pl.when(j >= rope_tiles)
        def _():
            o_ref[...] = acc.astype(o_ref.dtype)
    else:
        raise ValueError(epilogue)


def _matmul(x, w, *, tm, tn, out_dtype, epilogue="cast", extra=(), rope_tiles=0, name="matmul"):
    m, k = x.shape
    _, n = w.shape
    tm, tn = min(tm, m), min(tn, n)
    in_specs = [pl.BlockSpec((tm, k), lambda i, j: (i, 0)), pl.BlockSpec((k, tn), lambda i, j: (0, j))]
    if epilogue == "residual":
        in_specs.append(pl.BlockSpec((tm, tn), lambda i, j: (i, j)))
    elif epilogue == "rope":
        in_specs += [pl.BlockSpec((tm, V7X_LANES), lambda i, j: (i, 0))] * 2
    return pl.pallas_call(
        functools.partial(_mm_kernel, epilogue=epilogue, rope_tiles=rope_tiles),
        grid=(m // tm, n // tn),
        in_specs=in_specs,
        out_specs=pl.BlockSpec((tm, tn), lambda i, j: (i, j)),
        out_shape=jax.ShapeDtypeStruct((m, n), out_dtype),
        compiler_params=_params(("parallel", "parallel")),
        name=name,
    )(x, w, *extra)


def _ffn_kernel(xn_ref, wg_ref, wu_ref, cg_ref, cu_ref, wd_ref, res_ref, o_ref, carry_g, carry_u):
    i, j = pl.program_id(0), pl.program_id(1)
    tm = xn_ref.shape[0]
    xn = xn_ref[...]

    @pl.when(i == 0)
    def _():
        carry_g[j] = jnp.zeros(carry_g.shape[1:], F32)
        carry_u[j] = jnp.zeros(carry_u.shape[1:], F32)

    def conv(a, carry, c_ref):
        prev = carry[j]
        carry[j] = a[tm - V7X_SUBLANES:, :]
        ext = jnp.concatenate([prev, a], axis=0)
        a1 = ext[V7X_SUBLANES - 1:V7X_SUBLANES - 1 + tm, :]
        a2 = ext[V7X_SUBLANES - 2:V7X_SUBLANES - 2 + tm, :]
        c = c_ref[...]
        return c[3:4, :] + c[0:1, :] * a2 + c[1:2, :] * a1 + c[2:3, :] * a

    gate = conv(jnp.dot(xn, wg_ref[...], preferred_element_type=F32), carry_g, cg_ref)
    up = conv(jnp.dot(xn, wu_ref[...], preferred_element_type=F32), carry_u, cu_ref)
    act = (_gelu(gate) * up).astype(BF16)
    contrib = jnp.dot(act, wd_ref[...], preferred_element_type=F32)

    @pl.when(j == 0)
    def _():
        o_ref[...] = res_ref[...] + contrib

    @pl.when(j > 0)
    def _():
        o_ref[...] += contrib


def _conv_ffn(xn, res, w_up, conv_tab, w_down, *, tm=512, tf=256):
    s, d = xn.shape
    f = w_down.shape[0]
    nf = f // tf
    return pl.pallas_call(
        _ffn_kernel,
        grid=(s // tm, nf),
        in_specs=[
            pl.BlockSpec((tm, d), lambda i, j: (i, 0)),
            pl.BlockSpec((d, tf), lambda i, j: (0, j)),
            pl.BlockSpec((d, tf), lambda i, j: (0, j + nf)),
            pl.BlockSpec((V7X_SUBLANES, tf), lambda i, j: (0, j)),
            pl.BlockSpec((V7X_SUBLANES, tf), lambda i, j: (0, j + nf)),
            pl.BlockSpec((tf, d), lambda i, j: (j, 0)),
            pl.BlockSpec((tm, d), lambda i, j: (i, 0), pipeline_mode=pl.Buffered(1)),
        ],
        out_specs=pl.BlockSpec((tm, d), lambda i, j: (i, 0)),
        out_shape=jax.ShapeDtypeStruct((s, d), F32),
        scratch_shapes=[pltpu.VMEM((nf, V7X_SUBLANES, tf), F32), pltpu.VMEM((nf, V7X_SUBLANES, tf), F32)],
        compiler_params=_params(("arbitrary", "arbitrary")),
        name="conv_ffn",
    )(xn, w_up, w_up, conv_tab, conv_tab, w_down, res)


def _gmlp_gate_kernel(z_ref, lng_ref, lnb_ref, ws_ref, bs_ref, o_ref):
    rows = z_ref.shape[0]
    d = o_ref.shape[1]
    gd = d // GMLP_GROUPS
    v = z_ref[:, d:].astype(F32)
    mu = jnp.mean(v, axis=-1, keepdims=True)
    vc = v - mu
    var = jnp.mean(vc * vc, axis=-1, keepdims=True)
    vln = ((vc * lax.rsqrt(var + LN_EPS)) * lng_ref[...] + lnb_ref[...]).astype(BF16)
    t_idx = lax.broadcasted_iota(jnp.int32, (GMLP_CHUNK, GMLP_CHUNK), 0)
    s_idx = lax.broadcasted_iota(jnp.int32, (GMLP_CHUNK, GMLP_CHUNK), 1)
    bs = bs_ref[...]
    for g in range(GMLP_GROUPS):
        w = jnp.where(s_idx <= t_idx, ws_ref[g], 0.0).astype(BF16)
        bias = bs[:, g:g + 1]
        for c in range(rows // GMLP_CHUNK):
            r0 = c * GMLP_CHUNK
            sv = jnp.dot(w, vln[r0:r0 + GMLP_CHUNK, g * gd:(g + 1) * gd], preferred_element_type=F32) + bias
            u = z_ref[r0:r0 + GMLP_CHUNK, g * gd:(g + 1) * gd].astype(F32)
            o_ref[r0:r0 + GMLP_CHUNK, g * gd:(g + 1) * gd] = (u * sv).astype(o_ref.dtype)


def _gmlp_gate(z, ln_g, ln_b, w_s, b_s, rows=256):
    s, d2 = z.shape
    d = d2 // 2
    return pl.pallas_call(
        _gmlp_gate_kernel,
        grid=(s // rows,),
        in_specs=[
            pl.BlockSpec((rows, d2), lambda i: (i, 0)),
            pl.BlockSpec((1, d), lambda i: (0, 0)),
            pl.BlockSpec((1, d), lambda i: (0, 0)),
            pl.BlockSpec(w_s.shape, lambda i: (0, 0, 0)),
            pl.BlockSpec((GMLP_CHUNK, GMLP_GROUPS), lambda i: (0, 0)),
        ],
        out_specs=pl.BlockSpec((rows, d), lambda i: (i, 0)),
        out_shape=jax.ShapeDtypeStruct((s, d), BF16),
        compiler_params=_params(("parallel",)),
        name="gmlp_gate",
    )(z, ln_g.reshape(1, d), ln_b.reshape(1, d), w_s, b_s.T)


def _gmlp_mixer(x, xn, w_in, ln_g, ln_b, w_s, b_s, w_out):
    z = _matmul(xn, w_in, tm=1024, tn=1024, out_dtype=BF16, epilogue="gelu", name="gmlp_in")
    y = _gmlp_gate(z, ln_g, ln_b, w_s, b_s)
    return _matmul(y, w_out, tm=1024, tn=512, out_dtype=F32, epilogue="residual", extra=(x,), name="gmlp_out")


def _moba_kernel(q_ref, k_ref, v_ref, o_ref, kmean_ref, *, scale):
    i = pl.program_id(1)
    blk = q_ref.shape[0]
    dh = q_ref.shape[1]
    nb = k_ref.shape[0] // blk
    nbp = kmean_ref.shape[0]
    c_exp = scale * LOG2E

    @pl.when(i == 0)
    def _():
        kmean_ref[...] = jnp.zeros(kmean_ref.shape, F32)

        def mean_body(n, carry):
            kb = k_ref[pl.ds(pl.multiple_of(n * blk, blk), blk), :].astype(F32)
            kmean_ref[pl.ds(n, 1), :] = jnp.mean(kb, axis=0, keepdims=True)
            return carry

        lax.fori_loop(0, nb, mean_body, 0)

    q = q_ref[...]
    lane = lax.broadcasted_iota(jnp.int32, (blk, nbp), 1)
    gate = lax.dot_general(q, kmean_ref[...].astype(BF16), (((1,), (1,)), ((), ())), preferred_element_type=F32)
    valid = lane < i
    g = jnp.where(valid, gate, NEG_INF)
    sel = jnp.zeros((blk, nbp), jnp.bool_)
    for _ in range(MOBA_TOPK):
        m = jnp.max(g, axis=1, keepdims=True)
        idx = jnp.min(jnp.where(g == m, lane, nbp), axis=1, keepdims=True)
        hit = lane == idx
        sel = jnp.logical_or(sel, hit)
        g = jnp.where(hit, -jnp.inf, g)
    bias = jnp.where(jnp.logical_and(sel, valid), 0.0, NEG_INF).astype(BF16)
    q_aug = jnp.concatenate([q, bias], axis=1)

    r0 = pl.multiple_of(i * blk, blk)
    k_own = k_ref[pl.ds(r0, blk), :]
    s = lax.dot_general(q, k_own, (((1,), (1,)), ((), ())), preferred_element_type=F32)
    row = lax.broadcasted_iota(jnp.int32, (blk, blk), 0)
    col = lax.broadcasted_iota(jnp.int32, (blk, blk), 1)
    s = jnp.where(col <= row, s, NEG_INF)
    m0 = jnp.max(s, axis=1, keepdims=True)
    p = jnp.exp2((s - m0) * c_exp)
    l0 = jnp.sum(p, axis=1, keepdims=True)
    acc0 = jnp.dot(p.astype(BF16), v_ref[pl.ds(r0, blk), :], preferred_element_type=F32)

    code_lane = lax.broadcasted_iota(jnp.int32, (blk, nbp), 1)

    def body(n, carry):
        m_prev, l_prev, acc = carry
        kr = pl.multiple_of(n * blk, blk)
        k_aug = jnp.concatenate([k_ref[pl.ds(kr, blk), :], (code_lane == n).astype(BF16)], axis=1)
        sc = lax.dot_general(q_aug, k_aug, (((1,), (1,)), ((), ())), preferred_element_type=F32)
        m_new = jnp.maximum(m_prev, jnp.max(sc, axis=1, keepdims=True))
        alpha = jnp.exp2((m_prev - m_new) * c_exp)
        pn = jnp.exp2((sc - m_new) * c_exp)
        l_new = alpha * l_prev + jnp.sum(pn, axis=1, keepdims=True)
        acc_new = alpha * acc + jnp.dot(pn.astype(BF16), v_ref[pl.ds(kr, blk), :], preferred_element_type=F32)
        return m_new, l_new, acc_new

    _, l_fin, acc_fin = lax.fori_loop(0, i, body, (m0, l0, acc0))
    o_ref[...] = (acc_fin / l_fin).astype(o_ref.dtype)
    del dh


def _moba_attention(qkv, heads):
    s, d3 = qkv.shape
    d = d3 // 3
    dh = d // heads
    nb = s // MOBA_BLOCK
    nbp = -(-nb // V7X_LANES) * V7X_LANES
    return pl.pallas_call(
        functools.partial(_moba_kernel, scale=dh ** -0.5),
        grid=(heads, nb),
        in_specs=[
            pl.BlockSpec((MOBA_BLOCK, dh), lambda h, i: (i, h)),
            pl.BlockSpec((s, dh), lambda h, i: (0, heads + h)),
            pl.BlockSpec((s, dh), lambda h, i: (0, 2 * heads + h)),
        ],
        out_specs=pl.BlockSpec((MOBA_BLOCK, dh), lambda h, i: (i, h)),
        out_shape=jax.ShapeDtypeStruct((s, d), BF16),
        scratch_shapes=[pltpu.VMEM((nbp, dh), F32)],
        compiler_params=_params(("arbitrary", "arbitrary")),
        name="moba_attention",
    )(qkv, qkv, qkv)


def _rope_tables(s, dh):
    rope_dim = dh // 4
    inv = ROPE_THETA ** (-jnp.arange(0, rope_dim, 2, dtype=F32) / rope_dim)
    ang = jnp.arange(s).astype(F32)[:, None] * inv[None, :]
    cos, sin = jnp.cos(ang), jnp.sin(ang)
    pad = dh - rope_dim
    cos_t = jnp.concatenate([cos, cos, jnp.ones((s, pad), F32)], axis=1)
    sin_t = jnp.concatenate([-sin, sin, jnp.zeros((s, pad), F32)], axis=1)
    return cos_t, sin_t


def _moba_mixer(x, xn, w_qkv, w_out):
    s, d = x.shape
    dh = d // MOBA_HEADS
    tn = 1024
    cos_t, sin_t = _rope_tables(s, dh)
    qkv = _matmul(xn, w_qkv, tm=1024, tn=tn, out_dtype=BF16, epilogue="rope", extra=(cos_t, sin_t),
                  rope_tiles=2 * d // tn, name="moba_qkv")
    att = _moba_attention(qkv, MOBA_HEADS)
    return _matmul(att, w_out, tm=1024, tn=512, out_dtype=F32, epilogue="residual", extra=(x,), name="moba_out")


def _mlstm_kernel(q_ref, k_ref, v_ref, og_ref, gc_ref, gr_ref, b_ref, o_ref, c_ref, m_ref):
    ci = pl.program_id(1)
    L, dk = q_ref.shape
    dv = v_ref.shape[1]

    @pl.when(ci == 0)
    def _():
        c_ref[...] = jnp.zeros(c_ref.shape, F32)
        m_ref[...] = jnp.zeros(m_ref.shape, F32)

    b = b_ref[0]
    b_i, b_f = b[:, 0:1], b[:, 1:2]
    gc = gc_ref[0]
    gr = gr_ref[0]
    li_col = gc[:, 0:1] + b_i
    lf_col = jax.nn.log_sigmoid(gc[:, 1:2] + b_f)
    li_row = gr[0:1, :] + b_i
    lf_row = jax.nn.log_sigmoid(gr[1:2, :] + b_f)

    t_idx = lax.broadcasted_iota(jnp.int32, (L, L), 0)
    s_idx = lax.broadcasted_iota(jnp.int32, (L, L), 1)
    causal = s_idx <= t_idx
    b_col = jnp.sum(jnp.where(causal, lf_row, 0.0), axis=1, keepdims=True)
    b_row = jnp.sum(jnp.where(t_idx <= s_idx, lf_col, 0.0), axis=0, keepdims=True)
    b_last = b_row[:, L - 1:L]
    m_prev = m_ref[0:1, 0:1]

    dmat = jnp.where(causal, b_col - b_row + li_row, NEG_INF)
    inter = b_col + m_prev
    m_t = jnp.maximum(inter, jnp.max(dmat, axis=1, keepdims=True))
    q = q_ref[...]
    k = k_ref[...]
    qk = lax.dot_general(q, k, (((1,), (1,)), ((), ())), preferred_element_type=F32) * (dk ** -0.5)
    a = (jnp.exp(dmat - m_t) * qk).astype(BF16)
    s_inter = jnp.exp(inter - m_t)

    ones_col = (lax.broadcasted_iota(jnp.int32, (L, V7X_LANES), 1) == 0).astype(BF16)
    v_ext = jnp.concatenate([v_ref[...], ones_col], axis=1)
    c_prev = c_ref[...]
    qc = jnp.dot(q, c_prev.astype(BF16), preferred_element_type=F32) * (dk ** -0.5)
    num_ext = s_inter * qc + jnp.dot(a, v_ext, preferred_element_type=F32)
    den = num_ext[:, dv:dv + 1]
    h = num_ext[:, :dv] / jnp.maximum(jnp.abs(den), jnp.exp(-m_t))
    o_ref[...] = (jax.nn.sigmoid(og_ref[...].astype(F32)) * h).astype(o_ref.dtype)

    g_col = b_last - b_col + li_col
    m_new = jnp.maximum(b_last + m_prev, jnp.max(g_col, axis=0, keepdims=True))
    decay = jnp.exp(b_last + m_prev - m_new)
    wk = jnp.exp(g_col - m_new)
    kw = (k.astype(F32) * wk).astype(BF16)
    upd = lax.dot_general(kw, v_ext, (((0,), (0,)), ((), ())), preferred_element_type=F32)
    c_ref[...] = decay * c_prev + upd
    m_ref[...] = jnp.broadcast_to(m_new, m_ref.shape)


def _mlstm_scan(proj, gates_col, gates_row, bias, heads, chunk):
    s = proj.shape[0]
    dv = proj.shape[1] // (3 * heads)
    dk = dv // 2
    nc = s // chunk
    return pl.pallas_call(
        _mlstm_kernel,
        grid=(heads, nc),
        in_specs=[
            pl.BlockSpec((chunk, dk), lambda h, c: (c, h)),
            pl.BlockSpec((chunk, dk), lambda h, c: (c, heads + h)),
            pl.BlockSpec((chunk, dv), lambda h, c: (c, heads + h)),
            pl.BlockSpec((chunk, dv), lambda h, c: (c, 2 * heads + h)),
            pl.BlockSpec((1, chunk, 2), lambda h, c: (h, c, 0)),
            pl.BlockSpec((1, 2, chunk), lambda h, c: (h, 0, c)),
            pl.BlockSpec((1, 1, 2), lambda h, c: (h, 0, 0)),
        ],
        out_specs=pl.BlockSpec((chunk, dv), lambda h, c: (c, h)),
        out_shape=jax.ShapeDtypeStruct((s, heads * dv), BF16),
        scratch_shapes=[pltpu.VMEM((dk, dv + V7X_LANES), F32), pltpu.VMEM((V7X_SUBLANES, V7X_LANES), F32)],
        compiler_params=_params(("arbitrary", "arbitrary")),
        name="mlstm_scan",
    )(proj, proj, proj, proj, gates_col, gates_row, bias)


def _mlstm_mixer(x, xn, w_in, w_gates, b_i, b_f, w_out, chunk=256):
    s, d = x.shape
    h = MLSTM_HEADS
    proj = _matmul(xn, w_in, tm=1024, tn=1024, out_dtype=BF16, name="mlstm_in")
    gates = _matmul(xn, w_gates, tm=1024, tn=V7X_LANES, out_dtype=F32, name="mlstm_gates")
    pre = jnp.stack([gates[:, :h], gates[:, h:2 * h]], axis=0)
    gates_col = pre.transpose(2, 1, 0)
    gates_row = pre.transpose(2, 0, 1)
    bias = jnp.stack([b_i, b_f], axis=-1).reshape(h, 1, 2)
    hs = _mlstm_scan(proj, gates_col, gates_row, bias, h, chunk)
    return _matmul(hs, w_out, tm=1024, tn=512, out_dtype=F32, epilogue="residual", extra=(x,), name="mlstm_out")


def _ffn_block(x, norm_g, w_up, conv_w, conv_b, w_down):
    xn = _rmsnorm(x, norm_g, BF16)
    pad = jnp.zeros((V7X_SUBLANES - CONV_W - 1, conv_w.shape[1]), F32)
    conv_tab = jnp.concatenate([conv_w, conv_b[None, :], pad], axis=0)
    return _conv_ffn(xn, x, w_up.astype(BF16), conv_tab, w_down.astype(BF16))


def kernel(x, l0_norm_mix, l0_a_w_in, l0_a_ln_g, l0_a_ln_b, l0_a_w_s, l0_a_b_s, l0_a_w_out, l0_norm_ffn, l0_ffn_w_up, l0_ffn_conv_w, l0_ffn_conv_b, l0_ffn_w_down, l1_norm_mix, l1_b_w_qkv, l1_b_w_out, l1_norm_ffn, l1_ffn_w_up, l1_ffn_conv_w, l1_ffn_conv_b, l1_ffn_w_down, l2_norm_mix, l2_c_w_in, l2_c_b_i, l2_c_b_f, l2_c_w_out, l2_norm_ffn, l2_ffn_w_up, l2_ffn_conv_w, l2_ffn_conv_b, l2_ffn_w_down, l3_norm_mix, l3_a_w_in, l3_a_ln_g, l3_a_ln_b, l3_a_w_s, l3_a_b_s, l3_a_w_out, l3_norm_ffn, l3_ffn_w_up, l3_ffn_conv_w, l3_ffn_conv_b, l3_ffn_w_down, final_norm):
    batch, s, d = x.shape
    outs = []
    for bidx in range(batch):
        h = x[bidx]
        hn = _rmsnorm(h, l0_norm_mix, BF16)
        h = _gmlp_mixer(h, hn, l0_a_w_in.astype(BF16), l0_a_ln_g, l0_a_ln_b, l0_a_w_s, l0_a_b_s, l0_a_w_out.astype(BF16))
        h = _ffn_block(h, l0_norm_ffn, l0_ffn_w_up, l0_ffn_conv_w, l0_ffn_conv_b, l0_ffn_w_down)
        hn = _rmsnorm(h, l1_norm_mix, BF16)
        h = _moba_mixer(h, hn, l1_b_w_qkv.astype(BF16), l1_b_w_out.astype(BF16))
        h = _ffn_block(h, l1_norm_ffn, l1_ffn_w_up, l1_ffn_conv_w, l1_ffn_conv_b, l1_ffn_w_down)
        hn = _rmsnorm(h, l2_norm_mix, BF16)
        n_main = l2_c_w_in.shape[1] - 2 * MLSTM_HEADS
        w_main = l2_c_w_in[:, :n_main].astype(BF16)
        w_gates = jnp.pad(l2_c_w_in[:, n_main:], ((0, 0), (0, V7X_LANES - 2 * MLSTM_HEADS))).astype(BF16)
        h = _mlstm_mixer(h, hn, w_main, w_gates, l2_c_b_i, l2_c_b_f, l2_c_w_out.astype(BF16))
        h = _ffn_block(h, l2_norm_ffn, l2_ffn_w_up, l2_ffn_conv_w, l2_ffn_conv_b, l2_ffn_w_down)
        hn = _rmsnorm(h, l3_norm_mix, BF16)
        h = _gmlp_mixer(h, hn, l3_a_w_in.astype(BF16), l3_a_ln_g, l3_a_ln_b, l3_a_w_s, l3_a_b_s, l3_a_w_out.astype(BF16))
        h = _ffn_block(h, l3_norm_ffn, l3_ffn_w_up, l3_ffn_conv_w, l3_ffn_conv_b, l3_ffn_w_down)
        outs.append(_rmsnorm(h, final_norm, F32))
    return jnp.stack(outs, axis=0)
```

```python
import functools
import math

import jax
import jax.numpy as jnp
from jax import lax
from jax.experimental import pallas as pl
from jax.experimental.pallas import tpu as pltpu

F32 = jnp.float32
BF16 = jnp.bfloat16

V7X_LANES = 128
V7X_SUBLANES = 8
V7X_VMEM_BYTES = 64 * 1024 * 1024
VMEM_LIMIT_BYTES = V7X_VMEM_BYTES - 8 * 1024 * 1024

GMLP_CHUNK = 128
GMLP_GROUPS = 8
MOBA_HEADS = 32
MOBA_BLOCK = 256
MOBA_TOPK = 3
ROPE_THETA = 500000.0
MLSTM_HEADS = 8
CONV_W = 3
NEG_INF = -1e30
RMS_EPS = 1e-6
LN_EPS = 1e-5
LOG2E = math.log2(math.e)
NORM_ROWS = 64


def _params(semantics):
    return pltpu.CompilerParams(dimension_semantics=semantics, vmem_limit_bytes=VMEM_LIMIT_BYTES)


def _rmsnorm_kernel(x_ref, g_ref, o_ref):
    x = x_ref[...]
    y = x * lax.rsqrt(jnp.mean(x * x, axis=-1, keepdims=True) + RMS_EPS)
    o_ref[...] = (y * g_ref[...]).astype(o_ref.dtype)


def _rmsnorm(x, g, out_dtype, rows=256):
    s, d = x.shape
    return pl.pallas_call(
        _rmsnorm_kernel,
        grid=(s // rows,),
        in_specs=[pl.BlockSpec((rows, d), lambda i: (i, 0)), pl.BlockSpec((1, d), lambda i: (0, 0))],
        out_specs=pl.BlockSpec((rows, d), lambda i: (i, 0)),
        out_shape=jax.ShapeDtypeStruct((s, d), out_dtype),
        compiler_params=_params(("parallel",)),
        name="rmsnorm",
    )(x, g.reshape(1, d))


def _gelu(x):
    return 0.5 * x * (1.0 + lax.erf(x * math.sqrt(0.5)))


def _rope_heads(acc, cos, sin):
    rows, width = acc.shape
    lane = lax.broadcasted_iota(jnp.int32, (rows, V7X_LANES), 1)
    half = cos.shape[-1] // 8
    outs = []
    for h in range(width // V7X_LANES):
        xh = acc[:, h * V7X_LANES:(h + 1) * V7X_LANES]
        partner = jnp.where(lane < half, pltpu.roll(xh, V7X_LANES - half, axis=1), pltpu.roll(xh, half, axis=1))
        outs.append(xh * cos + partner * sin)
    return jnp.concatenate(outs, axis=1) if len(outs) > 1 else outs[0]


def _mm_kernel(*refs, epilogue, rope_tiles):
    x_ref, w_ref = refs[0], refs[1]
    o_ref = refs[-1]
    acc = jnp.dot(x_ref[...], w_ref[...], preferred_element_type=F32)
    if epilogue == "cast":
        o_ref[...] = acc.astype(o_ref.dtype)
    elif epilogue == "gelu":
        o_ref[...] = _gelu(acc).astype(o_ref.dtype)
    elif epilogue == "residual":
        o_ref[...] = (refs[2][...] + acc).astype(o_ref.dtype)
    elif epilogue == "rope":
        cos_ref, sin_ref = refs[2], refs[3]
        j = pl.program_id(1)

        @pl.when(j < rope_tiles)
        def _():
            o_ref[...] = _rope_heads(acc, cos_ref[...], sin_ref[...]).astype(o_ref.dtype)

        @pl.when(j >= rope_tiles)
        def _():
            o_ref[...] = acc.astype(o_ref.dtype)
    else:
        raise ValueError(epilogue)


def _matmul(x, w, *, tm, tn, out_dtype, epilogue="cast", extra=(), rope_tiles=0, name="matmul"):
    m, k = x.shape
    _, n = w.shape
    tm, tn = min(tm, m), min(tn, n)
    in_specs = [pl.BlockSpec((tm, k), lambda i, j: (i, 0)), pl.BlockSpec((k, tn), lambda i, j: (0, j))]
    if epilogue == "residual":
        in_specs.append(pl.BlockSpec((tm, tn), lambda i, j: (i, j)))
    elif epilogue == "rope":
        in_specs += [pl.BlockSpec((tm, V7X_LANES), lambda i, j: (i, 0))] * 2
    return pl.pallas_call(
        functools.partial(_mm_kernel, epilogue=epilogue, rope_tiles=rope_tiles),
        grid=(m // tm, n // tn),
        in_specs=in_specs,
        out_specs=pl.BlockSpec((tm, tn), lambda i, j: (i, j)),
        out_shape=jax.ShapeDtypeStruct((m, n), out_dtype),
        compiler_params=_params(("parallel", "parallel")),
        name=name,
    )(x, w, *extra)


def _ffn_kernel(x_ref, g_ref, wg_ref, wu_ref, cg_ref, cu_ref, wd_ref, o_ref, xn_ref, carry_g, carry_u, *, sub):
    i, j = pl.program_id(0), pl.program_id(1)
    tm = x_ref.shape[0]
    tf = wg_ref.shape[1]

    @pl.when(j == 0)
    def _():
        def norm_rows(c, carry):
            rows = pl.ds(pl.multiple_of(c * NORM_ROWS, NORM_ROWS), NORM_ROWS)
            x = x_ref[rows, :]
            y = x * lax.rsqrt(jnp.mean(x * x, axis=-1, keepdims=True) + RMS_EPS)
            xn_ref[rows, :] = (y * g_ref[...]).astype(BF16)
            o_ref[rows, :] = x
            return carry

        lax.fori_loop(0, tm // NORM_ROWS, norm_rows, 0)

    @pl.when(i == 0)
    def _():
        carry_g[j] = jnp.zeros(carry_g.shape[1:], F32)
        carry_u[j] = jnp.zeros(carry_u.shape[1:], F32)

    def conv(a, carry, c_ref, lanes):
        prev = carry[j, :, lanes]
        carry[j, :, lanes] = a[tm - V7X_SUBLANES:, :]
        ext = jnp.concatenate([prev, a], axis=0)
        a1 = ext[V7X_SUBLANES - 1:V7X_SUBLANES - 1 + tm, :]
        a2 = ext[V7X_SUBLANES - 2:V7X_SUBLANES - 2 + tm, :]
        c = c_ref[:, lanes]
        return c[3:4, :] + c[0:1, :] * a2 + c[1:2, :] * a1 + c[2:3, :] * a

    xn = xn_ref[...]
    pieces = [slice(k * sub, (k + 1) * sub) for k in range(tf // sub)]
    ups = [(jnp.dot(xn, wg_ref[:, p], preferred_element_type=F32), jnp.dot(xn, wu_ref[:, p], preferred_element_type=F32))
           for p in pieces]
    for p, (ag, au) in zip(pieces, ups):
        act = (_gelu(conv(ag, carry_g, cg_ref, p)) * conv(au, carry_u, cu_ref, p)).astype(BF16)
        o_ref[...] += jnp.dot(act, wd_ref[p, :], preferred_element_type=F32)


def _conv_ffn(x, norm_g, w_up, conv_tab, w_down, *, tm=512, tf=512, sub=256):
    s, d = x.shape
    f = w_down.shape[0]
    nf = f // tf
    return pl.pallas_call(
        functools.partial(_ffn_kernel, sub=sub),
        grid=(s // tm, nf),
        in_specs=[
            pl.BlockSpec((tm, d), lambda i, j: (i, 0), pipeline_mode=pl.Buffered(1)),
            pl.BlockSpec((1, d), lambda i, j: (0, 0)),
            pl.BlockSpec((d, tf), lambda i, j: (0, j)),
            pl.BlockSpec((d, tf), lambda i, j: (0, j + nf)),
            pl.BlockSpec((V7X_SUBLANES, tf), lambda i, j: (0, j)),
            pl.BlockSpec((V7X_SUBLANES, tf), lambda i, j: (0, j + nf)),
            pl.BlockSpec((tf, d), lambda i, j: (j, 0)),
        ],
        out_specs=pl.BlockSpec((tm, d), lambda i, j: (i, 0), pipeline_mode=pl.Buffered(1)),
        out_shape=jax.ShapeDtypeStruct((s, d), F32),
        scratch_shapes=[pltpu.VMEM((tm, d), BF16), pltpu.VMEM((nf, V7X_SUBLANES, tf), F32),
                        pltpu.VMEM((nf, V7X_SUBLANES, tf), F32)],
        compiler_params=_params(("arbitrary", "arbitrary")),
        name="conv_ffn",
    )(x, norm_g.reshape(1, d), w_up, w_up, conv_tab, conv_tab, w_down)


def _gmlp_gate_kernel(z_ref, lng_ref, lnb_ref, ws_ref, bs_ref, o_ref):
    rows = z_ref.shape[0]
    d = o_ref.shape[1]
    gd = d // GMLP_GROUPS
    v = z_ref[:, d:].astype(F32)
    mu = jnp.mean(v, axis=-1, keepdims=True)
    vc = v - mu
    var = jnp.mean(vc * vc, axis=-1, keepdims=True)
    vln = ((vc * lax.rsqrt(var + LN_EPS)) * lng_ref[...] + lnb_ref[...]).astype(BF16)
    t_idx = lax.broadcasted_iota(jnp.int32, (GMLP_CHUNK, GMLP_CHUNK), 0)
    s_idx = lax.broadcasted_iota(jnp.int32, (GMLP_CHUNK, GMLP_CHUNK), 1)
    bs = bs_ref[...]
    for g in range(GMLP_GROUPS):
        w = jnp.where(s_idx <= t_idx, ws_ref[g], 0.0).astype(BF16)
        bias = bs[:, g:g + 1]
        for c in range(rows // GMLP_CHUNK):
            r0 = c * GMLP_CHUNK
            sv = jnp.dot(w, vln[r0:r0 + GMLP_CHUNK, g * gd:(g + 1) * gd], preferred_element_type=F32) + bias
            u = z_ref[r0:r0 + GMLP_CHUNK, g * gd:(g + 1) * gd].astype(F32)
            o_ref[r0:r0 + GMLP_CHUNK, g * gd:(g + 1) * gd] = (u * sv).astype(o_ref.dtype)


def _gmlp_gate(z, ln_g, ln_b, w_s, b_s, rows=256):
    s, d2 = z.shape
    d = d2 // 2
    return pl.pallas_call(
        _gmlp_gate_kernel,
        grid=(s // rows,),
        in_specs=[
            pl.BlockSpec((rows, d2), lambda i: (i, 0)),
            pl.BlockSpec((1, d), lambda i: (0, 0)),
            pl.BlockSpec((1, d), lambda i: (0, 0)),
            pl.BlockSpec(w_s.shape, lambda i: (0, 0, 0)),
            pl.BlockSpec((GMLP_CHUNK, GMLP_GROUPS), lambda i: (0, 0)),
        ],
        out_specs=pl.BlockSpec((rows, d), lambda i: (i, 0)),
        out_shape=jax.ShapeDtypeStruct((s, d), BF16),
        compiler_params=_params(("parallel",)),
        name="gmlp_gate",
    )(z, ln_g.reshape(1, d), ln_b.reshape(1, d), w_s, b_s.T)


def _gmlp_mixer(x, xn, w_in, ln_g, ln_b, w_s, b_s, w_out):
    z = _matmul(xn, w_in, tm=1024, tn=1024, out_dtype=BF16, epilogue="gelu", name="gmlp_in")
    y = _gmlp_gate(z, ln_g, ln_b, w_s, b_s)
    return _matmul(y, w_out, tm=1024, tn=512, out_dtype=F32, epilogue="residual", extra=(x,), name="gmlp_out")


def _moba_select_kernel(q_ref, k_ref, o_ref, kmean_ref):
    r = pl.program_id(1)
    rows, _ = q_ref.shape
    blk = MOBA_BLOCK
    nb = k_ref.shape[0] // blk
    nbp = kmean_ref.shape[0]

    @pl.when(r == 0)
    def _():
        kmean_ref[...] = jnp.zeros(kmean_ref.shape, F32)

        def mean_body(n, carry):
            kb = k_ref[pl.ds(pl.multiple_of(n * blk, blk), blk), :]
            kmean_ref[pl.ds(n, 1), :] = jnp.mean(kb.astype(F32), axis=0, keepdims=True)
            return carry

        lax.fori_loop(0, nb, mean_body, 0)

    lane = lax.broadcasted_iota(jnp.int32, (rows, nbp), 1)
    own = (r * rows + lax.broadcasted_iota(jnp.int32, (rows, nbp), 0)) // blk
    gate = lax.dot_general(q_ref[...], kmean_ref[...].astype(BF16), (((1,), (1,)), ((), ())),
                           preferred_element_type=F32)
    valid = lane < own
    g = jnp.where(valid, gate, NEG_INF)
    sel = jnp.zeros((rows, nbp), jnp.bool_)
    for _ in range(MOBA_TOPK):
        m = jnp.max(g, axis=1, keepdims=True)
        idx = jnp.min(jnp.where(g == m, lane, nbp), axis=1, keepdims=True)
        hit = lane == idx
        sel = jnp.logical_or(sel, hit)
        g = jnp.where(hit, -jnp.inf, g)
    keep = jnp.logical_or(jnp.logical_and(sel, valid), lane == nb)
    o_ref[...] = jnp.where(keep, 0.0, NEG_INF).astype(o_ref.dtype)


def _moba_select(qkv, heads, rows=1024):
    s, d3 = qkv.shape
    dh = d3 // (3 * heads)
    nb = s // MOBA_BLOCK
    nbp = -(-(nb + 1) // V7X_LANES) * V7X_LANES
    rows = min(rows, s)
    return pl.pallas_call(
        _moba_select_kernel,
        grid=(heads, s // rows),
        in_specs=[
            pl.BlockSpec((rows, dh), lambda h, r: (r, h)),
            pl.BlockSpec((s, dh), lambda h, r: (0, heads + h)),
        ],
        out_specs=pl.BlockSpec((rows, nbp), lambda h, r: (r, h)),
        out_shape=jax.ShapeDtypeStruct((s, heads * nbp), BF16),
        scratch_shapes=[pltpu.VMEM((nbp, dh), F32)],
        compiler_params=_params(("arbitrary", "arbitrary")),
        name="moba_select",
    )(qkv, qkv)


def _moba_kernel(q_ref, b_ref, k_ref, v_ref, o_ref, kaug_ref, vext_ref, s_ref, *, scale, group):
    i = pl.program_id(1)
    blk, dh = q_ref.shape
    nb = k_ref.shape[0] // blk
    nbp = b_ref.shape[1]
    gk = group * blk
    c_exp = scale * LOG2E
    lane = lax.broadcasted_iota(jnp.int32, (blk, nbp), 1)

    @pl.when(i == 0)
    def _():
        def head_setup(n, carry):
            rows = pl.ds(pl.multiple_of(n * blk, blk), blk)
            kaug_ref[rows, :] = jnp.concatenate([k_ref[rows, :], (lane == n).astype(BF16)], axis=1)
            vext_ref[rows, :] = jnp.concatenate([v_ref[rows, :], (lane == 0).astype(BF16)], axis=1)
            return carry

        lax.fori_loop(0, nb, head_setup, 0)

    q_aug = jnp.concatenate([q_ref[...], b_ref[...]], axis=1)
    nt = (((1,), (1,)), ((), ()))

    own_row = pl.multiple_of(i * blk, blk)
    rest = (group - 1) * blk
    half = (group // 2) * blk
    k_own = jnp.concatenate([k_ref[pl.ds(own_row, blk), :], (lane == nb).astype(BF16)], axis=1)
    s_own = lax.dot_general(q_aug, k_own, nt, preferred_element_type=F32)
    row = lax.broadcasted_iota(jnp.int32, (blk, blk), 0)
    col = lax.broadcasted_iota(jnp.int32, (blk, blk), 1)
    s_ref[0, :, 0:blk] = jnp.where(col <= row, s_own, NEG_INF)
    s_ref[0, :, blk:half] = lax.dot_general(q_aug, kaug_ref[0:half - blk, :], nt, preferred_element_type=F32)
    s_ref[0, :, half:gk] = lax.dot_general(q_aug, kaug_ref[half - blk:rest, :], nt, preferred_element_type=F32)

    last_step = nb // group - 1

    def step(j, slot, m_prev, acc):
        ahead = jnp.minimum(j + 1, last_step)
        rows_ahead = pl.ds(pl.multiple_of((ahead * group - 1) * blk, blk), gk)
        s_ref[1 - slot] = lax.dot_general(q_aug, kaug_ref[rows_ahead, :], nt, preferred_element_type=F32)
        sc = s_ref[slot]
        m_new = jnp.maximum(m_prev, jnp.max(sc, axis=1, keepdims=True))
        alpha = jnp.exp2((m_prev - m_new) * c_exp)
        p = jnp.exp2((sc - m_new) * c_exp).astype(BF16)
        head_row = pl.multiple_of(jnp.where(j == 0, own_row, (j * group - 1) * blk), blk)
        tail_row = pl.multiple_of(jnp.where(j == 0, 0, j * gk), blk)
        pv = (jnp.dot(p[:, 0:blk], vext_ref[pl.ds(head_row, blk), :], preferred_element_type=F32)
              + jnp.dot(p[:, blk:gk], vext_ref[pl.ds(tail_row, rest), :], preferred_element_type=F32))
        return m_new, alpha * acc + pv

    def body(t, carry):
        m_mid, acc_mid = step(2 * t, 0, *carry)
        return step(2 * t + 1, 1, m_mid, acc_mid)

    init = (jnp.full((blk, 1), -jnp.inf, F32), jnp.zeros((blk, 2 * dh), F32))
    _, acc = lax.fori_loop(0, (i // group + 2) // 2, body, init)
    o_ref[...] = (acc[:, :dh] / acc[:, dh:dh + 1]).astype(o_ref.dtype)


def _moba_attention(qkv, bias, heads, group):
    s, d3 = qkv.shape
    d = d3 // 3
    dh = d // heads
    nb = s // MOBA_BLOCK
    nbp = bias.shape[1] // heads
    assert nb % (2 * group) == 0 and group % 2 == 0 and dh == V7X_LANES and nb < nbp
    return pl.pallas_call(
        functools.partial(_moba_kernel, scale=dh ** -0.5, group=group),
        grid=(heads, nb),
        in_specs=[
            pl.BlockSpec((MOBA_BLOCK, dh), lambda h, i: (i, h)),
            pl.BlockSpec((MOBA_BLOCK, nbp), lambda h, i: (i, h)),
            pl.BlockSpec((s, dh), lambda h, i: (0, heads + h)),
            pl.BlockSpec((s, dh), lambda h, i: (0, 2 * heads + h)),
        ],
        out_specs=pl.BlockSpec((MOBA_BLOCK, dh), lambda h, i: (i, h)),
        out_shape=jax.ShapeDtypeStruct((s, d), BF16),
        scratch_shapes=[pltpu.VMEM((s, dh + nbp), BF16), pltpu.VMEM((s, 2 * dh), BF16),
                        pltpu.VMEM((2, MOBA_BLOCK, group * MOBA_BLOCK), F32)],
        compiler_params=_params(("arbitrary", "arbitrary")),
        name="moba_attention",
    )(qkv, bias, qkv, qkv)


def _rope_tables(s, dh):
    rope_dim = dh // 4
    inv = ROPE_THETA ** (-jnp.arange(0, rope_dim, 2, dtype=F32) / rope_dim)
    ang = jnp.arange(s).astype(F32)[:, None] * inv[None, :]
    cos, sin = jnp.cos(ang), jnp.sin(ang)
    pad = dh - rope_dim
    cos_t = jnp.concatenate([cos, cos, jnp.ones((s, pad), F32)], axis=1)
    sin_t = jnp.concatenate([-sin, sin, jnp.zeros((s, pad), F32)], axis=1)
    return cos_t, sin_t


def _moba_mixer(x, xn, w_qkv, w_out, group=8):
    s, d = x.shape
    dh = d // MOBA_HEADS
    tn = 1024
    cos_t, sin_t = _rope_tables(s, dh)
    qkv = _matmul(xn, w_qkv, tm=1024, tn=tn, out_dtype=BF16, epilogue="rope", extra=(cos_t, sin_t),
                  rope_tiles=2 * d // tn, name="moba_qkv")
    att = _moba_attention(qkv, _moba_select(qkv, MOBA_HEADS), MOBA_HEADS, group)
    return _matmul(att, w_out, tm=1024, tn=512, out_dtype=F32, epilogue="residual", extra=(x,), name="moba_out")


def _mlstm_kernel(q_ref, k_ref, v_ref, og_ref, gc_ref, gr_ref, b_ref, o_ref, c_ref, m_ref):
    ci = pl.program_id(1)
    L, dk = q_ref.shape
    dv = v_ref.shape[1]

    @pl.when(ci == 0)
    def _():
        c_ref[...] = jnp.zeros(c_ref.shape, F32)
        m_ref[...] = jnp.zeros(m_ref.shape, F32)

    b = b_ref[0]
    b_i, b_f = b[:, 0:1], b[:, 1:2]
    gc = gc_ref[0]
    gr = gr_ref[0]
    li_col = gc[:, 0:1] + b_i
    lf_col = jax.nn.log_sigmoid(gc[:, 1:2] + b_f)
    li_row = gr[0:1, :] + b_i
    lf_row = jax.nn.log_sigmoid(gr[1:2, :] + b_f)

    t_idx = lax.broadcasted_iota(jnp.int32, (L, L), 0)
    s_idx = lax.broadcasted_iota(jnp.int32, (L, L), 1)
    causal = s_idx <= t_idx
    b_col = jnp.sum(jnp.where(causal, lf_row, 0.0), axis=1, keepdims=True)
    b_row = jnp.sum(jnp.where(t_idx <= s_idx, lf_col, 0.0), axis=0, keepdims=True)
    b_last = b_row[:, L - 1:L]
    m_prev = m_ref[0:1, 0:1]

    dmat = jnp.where(causal, b_col - b_row + li_row, NEG_INF)
    inter = b_col + m_prev
    m_t = jnp.maximum(inter, jnp.max(dmat, axis=1, keepdims=True))
    q = q_ref[...]
    k = k_ref[...]
    qk = lax.dot_general(q, k, (((1,), (1,)), ((), ())), preferred_element_type=F32) * (dk ** -0.5)
    a = (jnp.exp(dmat - m_t) * qk).astype(BF16)
    s_inter = jnp.exp(inter - m_t)

    ones_col = (lax.broadcasted_iota(jnp.int32, (L, V7X_LANES), 1) == 0).astype(BF16)
    v_ext = jnp.concatenate([v_ref[...], ones_col], axis=1)
    c_prev = c_ref[...]
    qc = jnp.dot(q, c_prev.astype(BF16), preferred_element_type=F32) * (dk ** -0.5)
    num_ext = s_inter * qc + jnp.dot(a, v_ext, preferred_element_type=F32)
    den = num_ext[:, dv:dv + 1]
    h = num_ext[:, :dv] / jnp.maximum(jnp.abs(den), jnp.exp(-m_t))
    o_ref[...] = (jax.nn.sigmoid(og_ref[...].astype(F32)) * h).astype(o_ref.dtype)

    g_col = b_last - b_col + li_col
    m_new = jnp.maximum(b_last + m_prev, jnp.max(g_col, axis=0, keepdims=True))
    decay = jnp.exp(b_last + m_prev - m_new)
    wk = jnp.exp(g_col - m_new)
    kw = (k.astype(F32) * wk).astype(BF16)
    upd = lax.dot_general(kw, v_ext, (((0,), (0,)), ((), ())), preferred_element_type=F32)
    c_ref[...] = decay * c_prev + upd
    m_ref[...] = jnp.broadcast_to(m_new, m_ref.shape)


def _mlstm_scan(proj, gates_col, gates_row, bias, heads, chunk):
    s = proj.shape[0]
    dv = proj.shape[1] // (3 * heads)
    dk = dv // 2
    nc = s // chunk
    return pl.pallas_call(
        _mlstm_kernel,
        grid=(heads, nc),
        in_specs=[
            pl.BlockSpec((chunk, dk), lambda h, c: (c, h)),
            pl.BlockSpec((chunk, dk), lambda h, c: (c, heads + h)),
            pl.BlockSpec((chunk, dv), lambda h, c: (c, heads + h)),
            pl.BlockSpec((chunk, dv), lambda h, c: (c, 2 * heads + h)),
            pl.BlockSpec((1, chunk, 2), lambda h, c: (h, c, 0)),
            pl.BlockSpec((1, 2, chunk), lambda h, c: (h, 0, c)),
            pl.BlockSpec((1, 1, 2), lambda h, c: (h, 0, 0)),
        ],
        out_specs=pl.BlockSpec((chunk, dv), lambda h, c: (c, h)),
        out_shape=jax.ShapeDtypeStruct((s, heads * dv), BF16),
        scratch_shapes=[pltpu.VMEM((dk, dv + V7X_LANES), F32), pltpu.VMEM((V7X_SUBLANES, V7X_LANES), F32)],
        compiler_params=_params(("arbitrary", "arbitrary")),
        name="mlstm_scan",
    )(proj, proj, proj, proj, gates_col, gates_row, bias)


def _mlstm_mixer(x, xn, w_in, w_gates, b_i, b_f, w_out, chunk=256):
    s, d = x.shape
    h = MLSTM_HEADS
    proj = _matmul(xn, w_in, tm=1024, tn=1024, out_dtype=BF16, name="mlstm_in")
    gates = _matmul(xn, w_gates, tm=1024, tn=V7X_LANES, out_dtype=F32, name="mlstm_gates")
    pre = jnp.stack([gates[:, :h], gates[:, h:2 * h]], axis=0)
    gates_col = pre.transpose(2, 1, 0)
    gates_row = pre.transpose(2, 0, 1)
    bias = jnp.stack([b_i, b_f], axis=-1).reshape(h, 1, 2)
    hs = _mlstm_scan(proj, gates_col, gates_row, bias, h, chunk)
    return _matmul(hs, w_out, tm=1024, tn=512, out_dtype=F32, epilogue="residual", extra=(x,), name="mlstm_out")


FFN_TILE = 512


def _ffn_block(x, norm_g, w_up, conv_w, conv_b, w_down):
    f = w_down.shape[0]
    fpad = -f % FFN_TILE

    def pad_halves(t):
        lead = [(0, 0)] * (t.ndim - 1)
        return jnp.concatenate([jnp.pad(t[..., :f], lead + [(0, fpad)]), jnp.pad(t[..., f:], lead + [(0, fpad)])], axis=-1)

    tab = jnp.concatenate([conv_w, conv_b[None, :], jnp.zeros((V7X_SUBLANES - CONV_W - 1, 2 * f), F32)], axis=0)
    return _conv_ffn(x, norm_g, pad_halves(w_up).astype(BF16), pad_halves(tab),
                     jnp.pad(w_down, ((0, fpad), (0, 0))).astype(BF16), tf=FFN_TILE)


def kernel(x, l0_norm_mix, l0_a_w_in, l0_a_ln_g, l0_a_ln_b, l0_a_w_s, l0_a_b_s, l0_a_w_out, l0_norm_ffn, l0_ffn_w_up, l0_ffn_conv_w, l0_ffn_conv_b, l0_ffn_w_down, l1_norm_mix, l1_b_w_qkv, l1_b_w_out, l1_norm_ffn, l1_ffn_w_up, l1_ffn_conv_w, l1_ffn_conv_b, l1_ffn_w_down, l2_norm_mix, l2_c_w_in, l2_c_b_i, l2_c_b_f, l2_c_w_out, l2_norm_ffn, l2_ffn_w_up, l2_ffn_conv_w, l2_ffn_conv_b, l2_ffn_w_down, l3_norm_mix, l3_a_w_in, l3_a_ln_g, l3_a_ln_b, l3_a_w_s, l3_a_b_s, l3_a_w_out, l3_norm_ffn, l3_ffn_w_up, l3_ffn_conv_w, l3_ffn_conv_b, l3_ffn_w_down, final_norm):
    batch, s, d = x.shape
    outs = []
    for bidx in range(batch):
        h = x[bidx]
        hn = _rmsnorm(h, l0_norm_mix, BF16)
        h = _gmlp_mixer(h, hn, l0_a_w_in.astype(BF16), l0_a_ln_g, l0_a_ln_b, l0_a_w_s, l0_a_b_s, l0_a_w_out.astype(BF16))
        h = _ffn_block(h, l0_norm_ffn, l0_ffn_w_up, l0_ffn_conv_w, l0_ffn_conv_b, l0_ffn_w_down)
        hn = _rmsnorm(h, l1_norm_mix, BF16)
        h = _moba_mixer(h, hn, l1_b_w_qkv.astype(BF16), l1_b_w_out.astype(BF16))
        h = _ffn_block(h, l1_norm_ffn, l1_ffn_w_up, l1_ffn_conv_w, l1_ffn_conv_b, l1_ffn_w_down)
        hn = _rmsnorm(h, l2_norm_mix, BF16)
        n_main = l2_c_w_in.shape[1] - 2 * MLSTM_HEADS
        w_main = l2_c_w_in[:, :n_main].astype(BF16)
        w_gates = jnp.pad(l2_c_w_in[:, n_main:], ((0, 0), (0, V7X_LANES - 2 * MLSTM_HEADS))).astype(BF16)
        h = _mlstm_mixer(h, hn, w_main, w_gates, l2_c_b_i, l2_c_b_f, l2_c_w_out.astype(BF16))
        h = _ffn_block(h, l2_norm_ffn, l2_ffn_w_up, l2_ffn_conv_w, l2_ffn_conv_b, l2_ffn_w_down)
        hn = _rmsnorm(h, l3_norm_mix, BF16)
        h = _gmlp_mixer(h, hn, l3_a_w_in.astype(BF16), l3_a_ln_g, l3_a_ln_b, l3_a_w_s, l3_a_b_s, l3_a_w_out.astype(BF16))
        h = _ffn_block(h, l3_norm_ffn, l3_ffn_w_up, l3_ffn_conv_w, l3_ffn_conv_b, l3_ffn_w_down)
        outs.append(_rmsnorm(h, final_norm, F32))
    return jnp.stack(outs, axis=0)
```

```python
import functools
import math

import jax
import jax.numpy as jnp
from jax import lax
from jax.experimental import pallas as pl
from jax.experimental.pallas import tpu as pltpu

F32 = jnp.float32
BF16 = jnp.bfloat16

V7X_LANES = 128
V7X_SUBLANES = 8
V7X_VMEM_BYTES = 64 * 1024 * 1024
VMEM_LIMIT_BYTES = V7X_VMEM_BYTES - 8 * 1024 * 1024

GMLP_CHUNK = 128
GMLP_GROUPS = 8
MOBA_HEADS = 32
MOBA_BLOCK = 256
MOBA_TOPK = 3
ROPE_THETA = 500000.0
MLSTM_HEADS = 8
CONV_W = 3
NEG_INF = -1e30
RMS_EPS = 1e-6
LN_EPS = 1e-5
LOG2E = math.log2(math.e)
NORM_ROWS = 64


def _params(semantics):
    return pltpu.CompilerParams(dimension_semantics=semantics, vmem_limit_bytes=VMEM_LIMIT_BYTES)


def _rmsnorm_kernel(x_ref, g_ref, o_ref):
    x = x_ref[...]
    y = x * lax.rsqrt(jnp.mean(x * x, axis=-1, keepdims=True) + RMS_EPS)
    o_ref[...] = (y * g_ref[...]).astype(o_ref.dtype)


def _rmsnorm(x, g, out_dtype, rows=256):
    s, d = x.shape
    return pl.pallas_call(
        _rmsnorm_kernel,
        grid=(s // rows,),
        in_specs=[pl.BlockSpec((rows, d), lambda i: (i, 0)), pl.BlockSpec((1, d), lambda i: (0, 0))],
        out_specs=pl.BlockSpec((rows, d), lambda i: (i, 0)),
        out_shape=jax.ShapeDtypeStruct((s, d), out_dtype),
        compiler_params=_params(("parallel",)),
        name="rmsnorm",
    )(x, g.reshape(1, d))


def _gelu(x):
    return 0.5 * x * (1.0 + lax.erf(x * math.sqrt(0.5)))


def _rope_heads(acc, cos, sin):
    rows, width = acc.shape
    lane = lax.broadcasted_iota(jnp.int32, (rows, V7X_LANES), 1)
    half = cos.shape[-1] // 8
    outs = []
    for h in range(width // V7X_LANES):
        xh = acc[:, h * V7X_LANES:(h + 1) * V7X_LANES]
        partner = jnp.where(lane < half, pltpu.roll(xh, V7X_LANES - half, axis=1), pltpu.roll(xh, half, axis=1))
        outs.append(xh * cos + partner * sin)
    return jnp.concatenate(outs, axis=1) if len(outs) > 1 else outs[0]


def _mm_kernel(*refs, epilogue, rope_tiles):
    x_ref, w_ref = refs[0], refs[1]
    o_ref = refs[-1]
    acc = jnp.dot(x_ref[...], w_ref[...], preferred_element_type=F32)
    if epilogue == "cast":
        o_ref[...] = acc.astype(o_ref.dtype)
    elif epilogue == "gelu":
        o_ref[...] = _gelu(acc).astype(o_ref.dtype)
    elif epilogue == "residual":
        o_ref[...] = (refs[2][...] + acc).astype(o_ref.dtype)
    elif epilogue == "rope":
        cos_ref, sin_ref = refs[2], refs[3]
        j = pl.program_id(1)

        @pl.when(j < rope_tiles)
        def _():
            o_ref[...] = _rope_heads(acc, cos_ref[...], sin_ref[...]).astype(o_ref.dtype)

        @pl.when(j >= rope_tiles)
        def _():
            o_ref[...] = acc.astype(o_ref.dtype)
    else:
        raise ValueError(epilogue)


def _matmul(x, w, *, tm, tn, out_dtype, epilogue="cast", extra=(), rope_tiles=0, name="matmul"):
    m, k = x.shape
    _, n = w.shape
    tm, tn = min(tm, m), min(tn, n)
    in_specs = [pl.BlockSpec((tm, k), lambda i, j: (i, 0)), pl.BlockSpec((k, tn), lambda i, j: (0, j))]
    if epilogue == "residual":
        in_specs.append(pl.BlockSpec((tm, tn), lambda i, j: (i, j)))
    elif epilogue == "rope":
        in_specs += [pl.BlockSpec((tm, V7X_LANES), lambda i, j: (i, 0))] * 2
    return pl.pallas_call(
        functools.partial(_mm_kernel, epilogue=epilogue, rope_tiles=rope_tiles),
        grid=(m // tm, n // tn),
        in_specs=in_specs,
        out_specs=pl.BlockSpec((tm, tn), lambda i, j: (i, j)),
        out_shape=jax.ShapeDtypeStruct((m, n), out_dtype),
        compiler_params=_params(("parallel", "parallel")),
        name=name,
    )(x, w, *extra)


def _ffn_kernel(x_ref, g_ref, wg_ref, wu_ref, cg_ref, cu_ref, wd_ref, o_ref, xn_ref, carry_g, carry_u, *, sub):
    i, j = pl.program_id(0), pl.program_id(1)
    tm = x_ref.shape[0]
    tf = wg_ref.shape[1]

    @pl.when(j == 0)
    def _():
        def norm_rows(c, carry):
            rows = pl.ds(pl.multiple_of(c * NORM_ROWS, NORM_ROWS), NORM_ROWS)
            x = x_ref[rows, :]
            y = x * lax.rsqrt(jnp.mean(x * x, axis=-1, keepdims=True) + RMS_EPS)
            xn_ref[rows, :] = (y * g_ref[...]).astype(BF16)
            o_ref[rows, :] = x
            return carry

        lax.fori_loop(0, tm // NORM_ROWS, norm_rows, 0)

    @pl.when(i == 0)
    def _():
        carry_g[j] = jnp.zeros(carry_g.shape[1:], F32)
        carry_u[j] = jnp.zeros(carry_u.shape[1:], F32)

    def conv(a, carry, c_ref, lanes):
        prev = carry[j, :, lanes]
        carry[j, :, lanes] = a[tm - V7X_SUBLANES:, :]
        ext = jnp.concatenate([prev, a], axis=0)
        a1 = ext[V7X_SUBLANES - 1:V7X_SUBLANES - 1 + tm, :]
        a2 = ext[V7X_SUBLANES - 2:V7X_SUBLANES - 2 + tm, :]
        c = c_ref[:, lanes]
        return c[3:4, :] + c[0:1, :] * a2 + c[1:2, :] * a1 + c[2:3, :] * a

    xn = xn_ref[...]
    pieces = [slice(k * sub, (k + 1) * sub) for k in range(tf // sub)]
    ups = [(jnp.dot(xn, wg_ref[:, p], preferred_element_type=F32), jnp.dot(xn, wu_ref[:, p], preferred_element_type=F32))
           for p in pieces]
    for p, (ag, au) in zip(pieces, ups):
        act = (_gelu(conv(ag, carry_g, cg_ref, p)) * conv(au, carry_u, cu_ref, p)).astype(BF16)
        o_ref[...] += jnp.dot(act, wd_ref[p, :], preferred_element_type=F32)


def _conv_ffn(x, norm_g, w_gate, w_up, tab_gate, tab_up, w_down, *, tm=512, sub=256):
    s, d = x.shape
    nf, _, tf = w_gate.shape
    return pl.pallas_call(
        functools.partial(_ffn_kernel, sub=sub),
        grid=(s // tm, nf),
        in_specs=[
            pl.BlockSpec((tm, d), lambda i, j: (i, 0), pipeline_mode=pl.Buffered(1)),
            pl.BlockSpec((1, d), lambda i, j: (0, 0)),
            pl.BlockSpec((None, d, tf), lambda i, j: (j, 0, 0)),
            pl.BlockSpec((None, d, tf), lambda i, j: (j, 0, 0)),
            pl.BlockSpec((V7X_SUBLANES, tf), lambda i, j: (0, j)),
            pl.BlockSpec((V7X_SUBLANES, tf), lambda i, j: (0, j)),
            pl.BlockSpec((tf, d), lambda i, j: (j, 0)),
        ],
        out_specs=pl.BlockSpec((tm, d), lambda i, j: (i, 0), pipeline_mode=pl.Buffered(1)),
        out_shape=jax.ShapeDtypeStruct((s, d), F32),
        scratch_shapes=[pltpu.VMEM((tm, d), BF16), pltpu.VMEM((nf, V7X_SUBLANES, tf), F32),
                        pltpu.VMEM((nf, V7X_SUBLANES, tf), F32)],
        compiler_params=_params(("arbitrary", "arbitrary")),
        name="conv_ffn",
    )(x, norm_g.reshape(1, d), w_gate, w_up, tab_gate, tab_up, w_down)


def _gmlp_gate_kernel(z_ref, lng_ref, lnb_ref, ws_ref, bs_ref, o_ref):
    rows = z_ref.shape[0]
    d = o_ref.shape[1]
    gd = d // GMLP_GROUPS
    v = z_ref[:, d:].astype(F32)
    mu = jnp.mean(v, axis=-1, keepdims=True)
    vc = v - mu
    var = jnp.mean(vc * vc, axis=-1, keepdims=True)
    vln = ((vc * lax.rsqrt(var + LN_EPS)) * lng_ref[...] + lnb_ref[...]).astype(BF16)
    t_idx = lax.broadcasted_iota(jnp.int32, (GMLP_CHUNK, GMLP_CHUNK), 0)
    s_idx = lax.broadcasted_iota(jnp.int32, (GMLP_CHUNK, GMLP_CHUNK), 1)
    bs = bs_ref[...]
    for g in range(GMLP_GROUPS):
        w = jnp.where(s_idx <= t_idx, ws_ref[g], 0.0).astype(BF16)
        bias = bs[:, g:g + 1]
        for c in range(rows // GMLP_CHUNK):
            r0 = c * GMLP_CHUNK
            sv = jnp.dot(w, vln[r0:r0 + GMLP_CHUNK, g * gd:(g + 1) * gd], preferred_element_type=F32) + bias
            u = z_ref[r0:r0 + GMLP_CHUNK, g * gd:(g + 1) * gd].astype(F32)
            o_ref[r0:r0 + GMLP_CHUNK, g * gd:(g + 1) * gd] = (u * sv).astype(o_ref.dtype)


def _gmlp_gate(z, ln_g, ln_b, w_s, b_s, rows=256):
    s, d2 = z.shape
    d = d2 // 2
    return pl.pallas_call(
        _gmlp_gate_kernel,
        grid=(s // rows,),
        in_specs=[
            pl.BlockSpec((rows, d2), lambda i: (i, 0)),
            pl.BlockSpec((1, d), lambda i: (0, 0)),
            pl.BlockSpec((1, d), lambda i: (0, 0)),
            pl.BlockSpec(w_s.shape, lambda i: (0, 0, 0)),
            pl.BlockSpec((GMLP_CHUNK, GMLP_GROUPS), lambda i: (0, 0)),
        ],
        out_specs=pl.BlockSpec((rows, d), lambda i: (i, 0)),
        out_shape=jax.ShapeDtypeStruct((s, d), BF16),
        compiler_params=_params(("parallel",)),
        name="gmlp_gate",
    )(z, ln_g.reshape(1, d), ln_b.reshape(1, d), w_s, b_s.T)


def _gmlp_mixer(x, xn, w_in, ln_g, ln_b, w_s, b_s, w_out):
    z = _matmul(xn, w_in, tm=1024, tn=1024, out_dtype=BF16, epilogue="gelu", name="gmlp_in")
    y = _gmlp_gate(z, ln_g, ln_b, w_s, b_s)
    return _matmul(y, w_out, tm=1024, tn=512, out_dtype=F32, epilogue="residual", extra=(x,), name="gmlp_out")


def _moba_select_kernel(q_ref, k_ref, o_ref, kmean_ref):
    r = pl.program_id(1)
    rows, _ = q_ref.shape
    blk = MOBA_BLOCK
    nb = k_ref.shape[0] // blk
    nbp = kmean_ref.shape[0]

    @pl.when(r == 0)
    def _():
        kmean_ref[...] = jnp.zeros(kmean_ref.shape, F32)

        def mean_body(n, carry):
            kb = k_ref[pl.ds(pl.multiple_of(n * blk, blk), blk), :]
            kmean_ref[pl.ds(n, 1), :] = jnp.mean(kb.astype(F32), axis=0, keepdims=True)
            return carry

        lax.fori_loop(0, nb, mean_body, 0)

    lane = lax.broadcasted_iota(jnp.int32, (rows, nbp), 1)
    own = (r * rows + lax.broadcasted_iota(jnp.int32, (rows, nbp), 0)) // blk
    gate = lax.dot_general(q_ref[...], kmean_ref[...].astype(BF16), (((1,), (1,)), ((), ())),
                           preferred_element_type=F32)
    valid = lane < own
    g = jnp.where(valid, gate, NEG_INF)
    sel = jnp.zeros((rows, nbp), jnp.bool_)
    for _ in range(MOBA_TOPK):
        m = jnp.max(g, axis=1, keepdims=True)
        idx = jnp.min(jnp.where(g == m, lane, nbp), axis=1, keepdims=True)
        hit = lane == idx
        sel = jnp.logical_or(sel, hit)
        g = jnp.where(hit, -jnp.inf, g)
    keep = jnp.logical_or(jnp.logical_and(sel, valid), lane == nb)
    o_ref[...] = jnp.where(keep, 0.0, NEG_INF).astype(o_ref.dtype)


def _moba_select(qkv, heads, rows=1024):
    s, d3 = qkv.shape
    dh = d3 // (3 * heads)
    nb = s // MOBA_BLOCK
    nbp = -(-(nb + 1) // V7X_LANES) * V7X_LANES
    rows = min(rows, s)
    return pl.pallas_call(
        _moba_select_kernel,
        grid=(heads, s // rows),
        in_specs=[
            pl.BlockSpec((rows, dh), lambda h, r: (r, h)),
            pl.BlockSpec((s, dh), lambda h, r: (0, heads + h)),
        ],
        out_specs=pl.BlockSpec((rows, nbp), lambda h, r: (r, h)),
        out_shape=jax.ShapeDtypeStruct((s, heads * nbp), BF16),
        scratch_shapes=[pltpu.VMEM((nbp, dh), F32)],
        compiler_params=_params(("arbitrary", "arbitrary")),
        name="moba_select",
    )(qkv, qkv)


def _moba_kernel(q_ref, b_ref, k_ref, v_ref, o_ref, kaug_ref, vext_ref, s_ref, *, scale, group):
    i = pl.program_id(1)
    blk, dh = q_ref.shape
    nb = k_ref.shape[0] // blk
    nbp = b_ref.shape[1]
    gk = group * blk
    c_exp = scale * LOG2E
    lane = lax.broadcasted_iota(jnp.int32, (blk, nbp), 1)

    @pl.when(i == 0)
    def _():
        def head_setup(n, carry):
            rows = pl.ds(pl.multiple_of(n * blk, blk), blk)
            kaug_ref[rows, :] = jnp.concatenate([k_ref[rows, :], (lane == n).astype(BF16)], axis=1)
            vext_ref[rows, :] = jnp.concatenate([v_ref[rows, :], (lane == 0).astype(BF16)], axis=1)
            return carry

        lax.fori_loop(0, nb, head_setup, 0)

    q_aug = jnp.concatenate([q_ref[...], b_ref[...]], axis=1)
    nt = (((1,), (1,)), ((), ()))

    own_row = pl.multiple_of(i * blk, blk)
    rest = (group - 1) * blk
    half = (group // 2) * blk
    k_own = jnp.concatenate([k_ref[pl.ds(own_row, blk), :], (lane == nb).astype(BF16)], axis=1)
    s_own = lax.dot_general(q_aug, k_own, nt, preferred_element_type=F32)
    row = lax.broadcasted_iota(jnp.int32, (blk, blk), 0)
    col = lax.broadcasted_iota(jnp.int32, (blk, blk), 1)
    s_ref[0, :, 0:blk] = jnp.where(col <= row, s_own, NEG_INF)
    s_ref[0, :, blk:half] = lax.dot_general(q_aug, kaug_ref[0:half - blk, :], nt, preferred_element_type=F32)
    s_ref[0, :, half:gk] = lax.dot_general(q_aug, kaug_ref[half - blk:rest, :], nt, preferred_element_type=F32)

    last_step = nb // group - 1

    def step(j, slot, m_prev, acc, look_ahead=True):
        if look_ahead:
            ahead = jnp.minimum(j + 1, last_step)
            rows_ahead = pl.ds(pl.multiple_of((ahead * group - 1) * blk, blk), gk)
            s_ref[1 - slot] = lax.dot_general(q_aug, kaug_ref[rows_ahead, :], nt, preferred_element_type=F32)
        sc = s_ref[slot]
        m_new = jnp.maximum(m_prev, jnp.max(sc, axis=1, keepdims=True))
        alpha = jnp.exp2((m_prev - m_new) * c_exp)
        p = jnp.exp2((sc - m_new) * c_exp).astype(BF16)
        head_row = pl.multiple_of(jnp.where(j == 0, own_row, (j * group - 1) * blk), blk)
        tail_row = pl.multiple_of(jnp.where(j == 0, 0, j * gk), blk)
        pv = (jnp.dot(p[:, 0:blk], vext_ref[pl.ds(head_row, blk), :], preferred_element_type=F32)
              + jnp.dot(p[:, blk:gk], vext_ref[pl.ds(tail_row, rest), :], preferred_element_type=F32))
        return m_new, alpha * acc + pv

    def body(t, carry):
        m_mid, acc_mid = step(2 * t, 0, *carry)
        return step(2 * t + 1, 1, m_mid, acc_mid)

    n_steps = i // group + 1
    init = (jnp.full((blk, 1), -jnp.inf, F32), jnp.zeros((blk, 2 * dh), F32))
    m_even, acc_even = lax.fori_loop(0, n_steps // 2, body, init)
    _, acc = lax.cond(n_steps % 2 == 1,
                      lambda: step(n_steps - 1, 0, m_even, acc_even, look_ahead=False),
                      lambda: (m_even, acc_even))
    o_ref[...] = (acc[:, :dh] / acc[:, dh:dh + 1]).astype(o_ref.dtype)


def _moba_attention(qkv, bias, heads, group):
    s, d3 = qkv.shape
    d = d3 // 3
    dh = d // heads
    nb = s // MOBA_BLOCK
    nbp = bias.shape[1] // heads
    assert nb % (2 * group) == 0 and group % 2 == 0 and dh == V7X_LANES and nb < nbp
    return pl.pallas_call(
        functools.partial(_moba_kernel, scale=dh ** -0.5, group=group),
        grid=(heads, nb),
        in_specs=[
            pl.BlockSpec((MOBA_BLOCK, dh), lambda h, i: (i, h)),
            pl.BlockSpec((MOBA_BLOCK, nbp), lambda h, i: (i, h)),
            pl.BlockSpec((s, dh), lambda h, i: (0, heads + h)),
            pl.BlockSpec((s, dh), lambda h, i: (0, 2 * heads + h)),
        ],
        out_specs=pl.BlockSpec((MOBA_BLOCK, dh), lambda h, i: (i, h)),
        out_shape=jax.ShapeDtypeStruct((s, d), BF16),
        scratch_shapes=[pltpu.VMEM((s, dh + nbp), BF16), pltpu.VMEM((s, 2 * dh), BF16),
                        pltpu.VMEM((2, MOBA_BLOCK, group * MOBA_BLOCK), F32)],
        compiler_params=_params(("arbitrary", "arbitrary")),
        name="moba_attention",
    )(qkv, bias, qkv, qkv)


def _rope_tables(s, dh):
    rope_dim = dh // 4
    inv = ROPE_THETA ** (-jnp.arange(0, rope_dim, 2, dtype=F32) / rope_dim)
    ang = jnp.arange(s).astype(F32)[:, None] * inv[None, :]
    cos, sin = jnp.cos(ang), jnp.sin(ang)
    pad = dh - rope_dim
    cos_t = jnp.concatenate([cos, cos, jnp.ones((s, pad), F32)], axis=1)
    sin_t = jnp.concatenate([-sin, sin, jnp.zeros((s, pad), F32)], axis=1)
    return cos_t, sin_t


def _moba_mixer(x, xn, w_qkv, w_out, group=8):
    s, d = x.shape
    dh = d // MOBA_HEADS
    tn = 1024
    cos_t, sin_t = _rope_tables(s, dh)
    qkv = _matmul(xn, w_qkv, tm=1024, tn=tn, out_dtype=BF16, epilogue="rope", extra=(cos_t, sin_t),
                  rope_tiles=2 * d // tn, name="moba_qkv")
    att = _moba_attention(qkv, _moba_select(qkv, MOBA_HEADS), MOBA_HEADS, group)
    return _matmul(att, w_out, tm=1024, tn=512, out_dtype=F32, epilogue="residual", extra=(x,), name="moba_out")


def _mlstm_kernel(q_ref, k_ref, v_ref, og_ref, gc_ref, gr_ref, b_ref, o_ref, c_ref, m_ref):
    ci = pl.program_id(1)
    L, dk = q_ref.shape
    dv = v_ref.shape[1]

    @pl.when(ci == 0)
    def _():
        c_ref[...] = jnp.zeros(c_ref.shape, F32)
        m_ref[...] = jnp.zeros(m_ref.shape, F32)

    b = b_ref[0]
    b_i, b_f = b[:, 0:1], b[:, 1:2]
    gc = gc_ref[0]
    gr = gr_ref[0]
    li_col = gc[:, 0:1] + b_i
    lf_col = jax.nn.log_sigmoid(gc[:, 1:2] + b_f)
    li_row = gr[0:1, :] + b_i
    lf_row = jax.nn.log_sigmoid(gr[1:2, :] + b_f)

    t_idx = lax.broadcasted_iota(jnp.int32, (L, L), 0)
    s_idx = lax.broadcasted_iota(jnp.int32, (L, L), 1)
    causal = s_idx <= t_idx
    b_col = jnp.sum(jnp.where(causal, lf_row, 0.0), axis=1, keepdims=True)
    b_row = jnp.sum(jnp.where(t_idx <= s_idx, lf_col, 0.0), axis=0, keepdims=True)
    b_last = b_row[:, L - 1:L]
    m_prev = m_ref[0:1, 0:1]

    dmat = jnp.where(causal, b_col - b_row + li_row, NEG_INF)
    inter = b_col + m_prev
    m_t = jnp.maximum(inter, jnp.max(dmat, axis=1, keepdims=True))
    q = q_ref[...]
    k = k_ref[...]
    qk = lax.dot_general(q, k, (((1,), (1,)), ((), ())), preferred_element_type=F32) * (dk ** -0.5)
    a = (jnp.exp(dmat - m_t) * qk).astype(BF16)
    s_inter = jnp.exp(inter - m_t)

    ones_col = (lax.broadcasted_iota(jnp.int32, (L, V7X_LANES), 1) == 0).astype(BF16)
    v_ext = jnp.concatenate([v_ref[...], ones_col], axis=1)
    c_prev = c_ref[...]
    qc = jnp.dot(q, c_prev.astype(BF16), preferred_element_type=F32) * (dk ** -0.5)
    num_ext = s_inter * qc + jnp.dot(a, v_ext, preferred_element_type=F32)
    den = num_ext[:, dv:dv + 1]
    h = num_ext[:, :dv] / jnp.maximum(jnp.abs(den), jnp.exp(-m_t))
    o_ref[...] = (jax.nn.sigmoid(og_ref[...].astype(F32)) * h).astype(o_ref.dtype)

    g_col = b_last - b_col + li_col
    m_new = jnp.maximum(b_last + m_prev, jnp.max(g_col, axis=0, keepdims=True))
    decay = jnp.exp(b_last + m_prev - m_new)
    wk = jnp.exp(g_col - m_new)
    kw = (k.astype(F32) * wk).astype(BF16)
    upd = lax.dot_general(kw, v_ext, (((0,), (0,)), ((), ())), preferred_element_type=F32)
    c_ref[...] = decay * c_prev + upd
    m_ref[...] = jnp.broadcast_to(m_new, m_ref.shape)


def _mlstm_scan(proj, gates_col, gates_row, bias, heads, chunk):
    s = proj.shape[0]
    dv = proj.shape[1] // (3 * heads)
    dk = dv // 2
    nc = s // chunk
    return pl.pallas_call(
        _mlstm_kernel,
        grid=(heads, nc),
        in_specs=[
            pl.BlockSpec((chunk, dk), lambda h, c: (c, h)),
            pl.BlockSpec((chunk, dk), lambda h, c: (c, heads + h)),
            pl.BlockSpec((chunk, dv), lambda h, c: (c, heads + h)),
            pl.BlockSpec((chunk, dv), lambda h, c: (c, 2 * heads + h)),
            pl.BlockSpec((1, chunk, 2), lambda h, c: (h, c, 0)),
            pl.BlockSpec((1, 2, chunk), lambda h, c: (h, 0, c)),
            pl.BlockSpec((1, 1, 2), lambda h, c: (h, 0, 0)),
        ],
        out_specs=pl.BlockSpec((chunk, dv), lambda h, c: (c, h)),
        out_shape=jax.ShapeDtypeStruct((s, heads * dv), BF16),
        scratch_shapes=[pltpu.VMEM((dk, dv + V7X_LANES), F32), pltpu.VMEM((V7X_SUBLANES, V7X_LANES), F32)],
        compiler_params=_params(("arbitrary", "arbitrary")),
        name="mlstm_scan",
    )(proj, proj, proj, proj, gates_col, gates_row, bias)


def _mlstm_mixer(x, xn, w_in, w_gates, b_i, b_f, w_out, chunk=256):
    s, d = x.shape
    h = MLSTM_HEADS
    proj = _matmul(xn, w_in, tm=1024, tn=1024, out_dtype=BF16, name="mlstm_in")
    gates = _matmul(xn, w_gates, tm=1024, tn=V7X_LANES, out_dtype=F32, name="mlstm_gates")
    pre = jnp.stack([gates[:, :h], gates[:, h:2 * h]], axis=0)
    gates_col = pre.transpose(2, 1, 0)
    gates_row = pre.transpose(2, 0, 1)
    bias = jnp.stack([b_i, b_f], axis=-1).reshape(h, 1, 2)
    hs = _mlstm_scan(proj, gates_col, gates_row, bias, h, chunk)
    return _matmul(hs, w_out, tm=1024, tn=512, out_dtype=F32, epilogue="residual", extra=(x,), name="mlstm_out")


FFN_TILE = 512


def _ffn_block(x, norm_g, w_up, conv_w, conv_b, w_down):
    f = w_down.shape[0]
    fpad = -f % FFN_TILE

    def halves(t, dtype):
        return [jnp.pad(t[:, lo:lo + f].astype(dtype), ((0, 0), (0, fpad))) for lo in (0, f)]

    def tile_major(t):
        return t.reshape(t.shape[0], -1, FFN_TILE).transpose(1, 0, 2)

    tab = jnp.concatenate([conv_w, conv_b[None, :], jnp.zeros((V7X_SUBLANES - CONV_W - 1, 2 * f), F32)], axis=0)
    w_gate, w_upp = halves(w_up, BF16)
    return _conv_ffn(x, norm_g, tile_major(w_gate), tile_major(w_upp), *halves(tab, F32),
                     jnp.pad(w_down.astype(BF16), ((0, fpad), (0, 0))))


def kernel(x, l0_norm_mix, l0_a_w_in, l0_a_ln_g, l0_a_ln_b, l0_a_w_s, l0_a_b_s, l0_a_w_out, l0_norm_ffn, l0_ffn_w_up, l0_ffn_conv_w, l0_ffn_conv_b, l0_ffn_w_down, l1_norm_mix, l1_b_w_qkv, l1_b_w_out, l1_norm_ffn, l1_ffn_w_up, l1_ffn_conv_w, l1_ffn_conv_b, l1_ffn_w_down, l2_norm_mix, l2_c_w_in, l2_c_b_i, l2_c_b_f, l2_c_w_out, l2_norm_ffn, l2_ffn_w_up, l2_ffn_conv_w, l2_ffn_conv_b, l2_ffn_w_down, l3_norm_mix, l3_a_w_in, l3_a_ln_g, l3_a_ln_b, l3_a_w_s, l3_a_b_s, l3_a_w_out, l3_norm_ffn, l3_ffn_w_up, l3_ffn_conv_w, l3_ffn_conv_b, l3_ffn_w_down, final_norm):
    batch, s, d = x.shape
    outs = []
    for bidx in range(batch):
        h = x[bidx]
        hn = _rmsnorm(h, l0_norm_mix, BF16)
        h = _gmlp_mixer(h, hn, l0_a_w_in.astype(BF16), l0_a_ln_g, l0_a_ln_b, l0_a_w_s, l0_a_b_s, l0_a_w_out.astype(BF16))
        h = _ffn_block(h, l0_norm_ffn, l0_ffn_w_up, l0_ffn_conv_w, l0_ffn_conv_b, l0_ffn_w_down)
        hn = _rmsnorm(h, l1_norm_mix, BF16)
        h = _moba_mixer(h, hn, l1_b_w_qkv.astype(BF16), l1_b_w_out.astype(BF16))
        h = _ffn_block(h, l1_norm_ffn, l1_ffn_w_up, l1_ffn_conv_w, l1_ffn_conv_b, l1_ffn_w_down)
        hn = _rmsnorm(h, l2_norm_mix, BF16)
        n_main = l2_c_w_in.shape[1] - 2 * MLSTM_HEADS
        w_main = l2_c_w_in[:, :n_main].astype(BF16)
        w_gates = jnp.pad(l2_c_w_in[:, n_main:], ((0, 0), (0, V7X_LANES - 2 * MLSTM_HEADS))).astype(BF16)
        h = _mlstm_mixer(h, hn, w_main, w_gates, l2_c_b_i, l2_c_b_f, l2_c_w_out.astype(BF16))
        h = _ffn_block(h, l2_norm_ffn, l2_ffn_w_up, l2_ffn_conv_w, l2_ffn_conv_b, l2_ffn_w_down)
        hn = _rmsnorm(h, l3_norm_mix, BF16)
        h = _gmlp_mixer(h, hn, l3_a_w_in.astype(BF16), l3_a_ln_g, l3_a_ln_b, l3_a_w_s, l3_a_b_s, l3_a_w_out.astype(BF16))
        h = _ffn_block(h, l3_norm_ffn, l3_ffn_w_up, l3_ffn_conv_w, l3_ffn_conv_b, l3_ffn_w_down)
        outs.append(_rmsnorm(h, final_norm, F32))
    return jnp.stack(outs, axis=0)
```

```python
import functools
import math

import jax
import jax.numpy as jnp
from jax import lax
from jax.experimental import pallas as pl
from jax.experimental.pallas import tpu as pltpu

F32 = jnp.float32
BF16 = jnp.bfloat16

V7X_LANES = 128
V7X_SUBLANES = 8
V7X_VMEM_BYTES = 64 * 1024 * 1024
VMEM_LIMIT_BYTES = V7X_VMEM_BYTES - 8 * 1024 * 1024

GMLP_CHUNK = 128
GMLP_GROUPS = 8
MOBA_HEADS = 32
MOBA_BLOCK = 256
MOBA_TOPK = 3
ROPE_THETA = 500000.0
MLSTM_HEADS = 8
CONV_W = 3
NEG_INF = -1e30
RMS_EPS = 1e-6
LN_EPS = 1e-5
LOG2E = math.log2(math.e)
NORM_ROWS = 64


def _params(semantics):
    return pltpu.CompilerParams(dimension_semantics=semantics, vmem_limit_bytes=VMEM_LIMIT_BYTES)


def _rmsnorm_kernel(x_ref, g_ref, o_ref):
    x = x_ref[...]
    y = x * lax.rsqrt(jnp.mean(x * x, axis=-1, keepdims=True) + RMS_EPS)
    o_ref[...] = (y * g_ref[...]).astype(o_ref.dtype)


def _rmsnorm(x, g, out_dtype, rows=256):
    s, d = x.shape
    return pl.pallas_call(
        _rmsnorm_kernel,
        grid=(s // rows,),
        in_specs=[pl.BlockSpec((rows, d), lambda i: (i, 0)), pl.BlockSpec((1, d), lambda i: (0, 0))],
        out_specs=pl.BlockSpec((rows, d), lambda i: (i, 0)),
        out_shape=jax.ShapeDtypeStruct((s, d), out_dtype),
        compiler_params=_params(("parallel",)),
        name="rmsnorm",
    )(x, g.reshape(1, d))


def _gelu(x):
    return 0.5 * x * (1.0 + lax.erf(x * math.sqrt(0.5)))


def _rope_heads(acc, cos, sin):
    rows, width = acc.shape
    lane = lax.broadcasted_iota(jnp.int32, (rows, V7X_LANES), 1)
    half = cos.shape[-1] // 8
    outs = []
    for h in range(width // V7X_LANES):
        xh = acc[:, h * V7X_LANES:(h + 1) * V7X_LANES]
        partner = jnp.where(lane < half, pltpu.roll(xh, V7X_LANES - half, axis=1), pltpu.roll(xh, half, axis=1))
        outs.append(xh * cos + partner * sin)
    return jnp.concatenate(outs, axis=1) if len(outs) > 1 else outs[0]


def _mm_kernel(*refs, epilogue, rope_tiles):
    x_ref, w_ref = refs[0], refs[1]
    o_ref = refs[-1]
    acc = jnp.dot(x_ref[...], w_ref[...].astype(BF16), preferred_element_type=F32)
    if epilogue == "cast":
        o_ref[...] = acc.astype(o_ref.dtype)
    elif epilogue == "gelu":
        o_ref[...] = _gelu(acc).astype(o_ref.dtype)
    elif epilogue == "residual":
        o_ref[...] = (refs[2][...] + acc).astype(o_ref.dtype)
    elif epilogue == "rope":
        cos_ref, sin_ref = refs[2], refs[3]
        j = pl.program_id(1)

        @pl.when(j < rope_tiles)
        def _():
            o_ref[...] = _rope_heads(acc, cos_ref[...], sin_ref[...]).astype(o_ref.dtype)

        @pl.when(j >= rope_tiles)
        def _():
            o_ref[...] = acc.astype(o_ref.dtype)
    else:
        raise ValueError(epilogue)


def _matmul(x, w, *, tm, tn, out_dtype, epilogue="cast", extra=(), rope_tiles=0, n=None, name="matmul"):
    m, k = x.shape
    n = n or w.shape[1]
    tm, tn = min(tm, m), min(tn, n)
    in_specs = [pl.BlockSpec((tm, k), lambda i, j: (i, 0)), pl.BlockSpec((k, tn), lambda i, j: (0, j))]
    if epilogue == "residual":
        in_specs.append(pl.BlockSpec((tm, tn), lambda i, j: (i, j)))
    elif epilogue == "rope":
        in_specs += [pl.BlockSpec((tm, V7X_LANES), lambda i, j: (i, 0))] * 2
    return pl.pallas_call(
        functools.partial(_mm_kernel, epilogue=epilogue, rope_tiles=rope_tiles),
        grid=(m // tm, n // tn),
        in_specs=in_specs,
        out_specs=pl.BlockSpec((tm, tn), lambda i, j: (i, j)),
        out_shape=jax.ShapeDtypeStruct((m, n), out_dtype),
        compiler_params=_params(("parallel", "parallel")),
        name=name,
    )(x, w, *extra)


def _cast_pad_kernel(w_ref, o_ref, *, valid_blocks):
    j = pl.program_id(0)

    @pl.when(j < valid_blocks)
    def _():
        o_ref[...] = w_ref[...].astype(o_ref.dtype)

    @pl.when(j >= valid_blocks)
    def _():
        o_ref[...] = jnp.zeros(o_ref.shape, o_ref.dtype)


def _cast_pad(w, axis, start, size, out_size, blk=256):
    assert start % blk == 0 and size % blk == 0 and out_size % blk == 0
    first, valid = start // blk, size // blk
    other = w.shape[1 - axis]
    if axis == 1:
        block = (other, blk)
        in_map = lambda j: (0, first + jnp.minimum(j, valid - 1))
        out_map = lambda j: (0, j)
        out_shape = (other, out_size)
    else:
        block = (blk, other)
        in_map = lambda j: (first + jnp.minimum(j, valid - 1), 0)
        out_map = lambda j: (j, 0)
        out_shape = (out_size, other)
    return pl.pallas_call(
        functools.partial(_cast_pad_kernel, valid_blocks=valid),
        grid=(out_size // blk,),
        in_specs=[pl.BlockSpec(block, in_map)],
        out_specs=pl.BlockSpec(block, out_map),
        out_shape=jax.ShapeDtypeStruct(out_shape, BF16),
        compiler_params=_params(("parallel",)),
        name="cast_pad",
    )(w)


def _ffn_kernel(x_ref, g_ref, wg_ref, wu_ref, cg_ref, cu_ref, wd_ref, o_ref, xn_ref, carry_g, carry_u, *, sub):
    i, j = pl.program_id(0), pl.program_id(1)
    tm = x_ref.shape[0]
    tf = wg_ref.shape[1]

    @pl.when(j == 0)
    def _():
        def norm_rows(c, carry):
            rows = pl.ds(pl.multiple_of(c * NORM_ROWS, NORM_ROWS), NORM_ROWS)
            x = x_ref[rows, :]
            y = x * lax.rsqrt(jnp.mean(x * x, axis=-1, keepdims=True) + RMS_EPS)
            xn_ref[rows, :] = (y * g_ref[...]).astype(BF16)
            o_ref[rows, :] = x
            return carry

        lax.fori_loop(0, tm // NORM_ROWS, norm_rows, 0)

    @pl.when(i == 0)
    def _():
        carry_g[j] = jnp.zeros(carry_g.shape[1:], F32)
        carry_u[j] = jnp.zeros(carry_u.shape[1:], F32)

    def conv(a, carry, c_ref, lanes):
        prev = carry[j, :, lanes]
        carry[j, :, lanes] = a[tm - V7X_SUBLANES:, :]
        ext = jnp.concatenate([prev, a], axis=0)
        a1 = ext[V7X_SUBLANES - 1:V7X_SUBLANES - 1 + tm, :]
        a2 = ext[V7X_SUBLANES - 2:V7X_SUBLANES - 2 + tm, :]
        c = c_ref[:, lanes]
        return c[3:4, :] + c[0:1, :] * a2 + c[1:2, :] * a1 + c[2:3, :] * a

    xn = xn_ref[...]
    pieces = [slice(k * sub, (k + 1) * sub) for k in range(tf // sub)]
    ups = [(jnp.dot(xn, wg_ref[:, p], preferred_element_type=F32), jnp.dot(xn, wu_ref[:, p], preferred_element_type=F32))
           for p in pieces]
    for p, (ag, au) in zip(pieces, ups):
        act = (_gelu(conv(ag, carry_g, cg_ref, p)) * conv(au, carry_u, cu_ref, p)).astype(BF16)
        o_ref[...] += jnp.dot(act, wd_ref[p, :], preferred_element_type=F32)


def _conv_ffn(x, norm_g, w_gate, w_up, tab_gate, tab_up, w_down, *, tm=512, tf=512, sub=256):
    s, d = x.shape
    nf = w_down.shape[0] // tf
    return pl.pallas_call(
        functools.partial(_ffn_kernel, sub=sub),
        grid=(s // tm, nf),
        in_specs=[
            pl.BlockSpec((tm, d), lambda i, j: (i, 0), pipeline_mode=pl.Buffered(1)),
            pl.BlockSpec((1, d), lambda i, j: (0, 0)),
            pl.BlockSpec((d, tf), lambda i, j: (0, j)),
            pl.BlockSpec((d, tf), lambda i, j: (0, j)),
            pl.BlockSpec((V7X_SUBLANES, tf), lambda i, j: (0, j)),
            pl.BlockSpec((V7X_SUBLANES, tf), lambda i, j: (0, j)),
            pl.BlockSpec((tf, d), lambda i, j: (j, 0)),
        ],
        out_specs=pl.BlockSpec((tm, d), lambda i, j: (i, 0), pipeline_mode=pl.Buffered(1)),
        out_shape=jax.ShapeDtypeStruct((s, d), F32),
        scratch_shapes=[pltpu.VMEM((tm, d), BF16), pltpu.VMEM((nf, V7X_SUBLANES, tf), F32),
                        pltpu.VMEM((nf, V7X_SUBLANES, tf), F32)],
        compiler_params=_params(("arbitrary", "arbitrary")),
        name="conv_ffn",
    )(x, norm_g.reshape(1, d), w_gate, w_up, tab_gate, tab_up, w_down)


def _gmlp_gate_kernel(z_ref, lng_ref, lnb_ref, ws_ref, bs_ref, o_ref):
    rows = z_ref.shape[0]
    d = o_ref.shape[1]
    gd = d // GMLP_GROUPS
    v = z_ref[:, d:].astype(F32)
    mu = jnp.mean(v, axis=-1, keepdims=True)
    vc = v - mu
    var = jnp.mean(vc * vc, axis=-1, keepdims=True)
    vln = ((vc * lax.rsqrt(var + LN_EPS)) * lng_ref[...] + lnb_ref[...]).astype(BF16)
    t_idx = lax.broadcasted_iota(jnp.int32, (GMLP_CHUNK, GMLP_CHUNK), 0)
    s_idx = lax.broadcasted_iota(jnp.int32, (GMLP_CHUNK, GMLP_CHUNK), 1)
    bs = bs_ref[...]
    for g in range(GMLP_GROUPS):
        w = jnp.where(s_idx <= t_idx, ws_ref[g], 0.0).astype(BF16)
        bias = bs[:, g:g + 1]
        for c in range(rows // GMLP_CHUNK):
            r0 = c * GMLP_CHUNK
            sv = jnp.dot(w, vln[r0:r0 + GMLP_CHUNK, g * gd:(g + 1) * gd], preferred_element_type=F32) + bias
            u = z_ref[r0:r0 + GMLP_CHUNK, g * gd:(g + 1) * gd].astype(F32)
            o_ref[r0:r0 + GMLP_CHUNK, g * gd:(g + 1) * gd] = (u * sv).astype(o_ref.dtype)


def _gmlp_gate(z, ln_g, ln_b, w_s, b_s, rows=256):
    s, d2 = z.shape
    d = d2 // 2
    return pl.pallas_call(
        _gmlp_gate_kernel,
        grid=(s // rows,),
        in_specs=[
            pl.BlockSpec((rows, d2), lambda i: (i, 0)),
            pl.BlockSpec((1, d), lambda i: (0, 0)),
            pl.BlockSpec((1, d), lambda i: (0, 0)),
            pl.BlockSpec(w_s.shape, lambda i: (0, 0, 0)),
            pl.BlockSpec((GMLP_CHUNK, GMLP_GROUPS), lambda i: (0, 0)),
        ],
        out_specs=pl.BlockSpec((rows, d), lambda i: (i, 0)),
        out_shape=jax.ShapeDtypeStruct((s, d), BF16),
        compiler_params=_params(("parallel",)),
        name="gmlp_gate",
    )(z, ln_g.reshape(1, d), ln_b.reshape(1, d), w_s, b_s.T)


def _gmlp_mixer(x, xn, w_in, ln_g, ln_b, w_s, b_s, w_out):
    z = _matmul(xn, w_in, tm=1024, tn=512, out_dtype=BF16, epilogue="gelu", name="gmlp_in")
    y = _gmlp_gate(z, ln_g, ln_b, w_s, b_s)
    return _matmul(y, w_out, tm=1024, tn=512, out_dtype=F32, epilogue="residual", extra=(x,), name="gmlp_out")


def _moba_select_kernel(q_ref, k_ref, o_ref, kmean_ref):
    r = pl.program_id(1)
    rows, _ = q_ref.shape
    blk = MOBA_BLOCK
    nb = k_ref.shape[0] // blk
    nbp = kmean_ref.shape[0]

    @pl.when(r == 0)
    def _():
        kmean_ref[...] = jnp.zeros(kmean_ref.shape, F32)

        def mean_body(n, carry):
            kb = k_ref[pl.ds(pl.multiple_of(n * blk, blk), blk), :]
            kmean_ref[pl.ds(n, 1), :] = jnp.mean(kb.astype(F32), axis=0, keepdims=True)
            return carry

        lax.fori_loop(0, nb, mean_body, 0)

    n_idx = lax.broadcasted_iota(jnp.int32, (nbp, rows), 0)
    own = (r * rows + lax.broadcasted_iota(jnp.int32, (nbp, rows), 1)) // blk
    gate = lax.dot_general(kmean_ref[...].astype(BF16), q_ref[...], (((1,), (1,)), ((), ())),
                           preferred_element_type=F32)
    valid = n_idx < own
    g = jnp.where(valid, gate, NEG_INF)
    sel = jnp.zeros((nbp, rows), jnp.bool_)
    for _ in range(MOBA_TOPK):
        m = jnp.max(g, axis=0, keepdims=True)
        idx = jnp.min(jnp.where(g == m, n_idx, nbp), axis=0, keepdims=True)
        hit = n_idx == idx
        sel = jnp.logical_or(sel, hit)
        g = jnp.where(hit, -jnp.inf, g)
    keep = jnp.logical_or(jnp.logical_and(sel, valid), n_idx == nb)
    o_ref[...] = jnp.where(keep, 0.0, NEG_INF).T.astype(o_ref.dtype)


def _moba_select(qkv, heads, rows=1024):
    s, d3 = qkv.shape
    dh = d3 // (3 * heads)
    nb = s // MOBA_BLOCK
    nbp = -(-(nb + 1) // V7X_LANES) * V7X_LANES
    rows = min(rows, s)
    return pl.pallas_call(
        _moba_select_kernel,
        grid=(heads, s // rows),
        in_specs=[
            pl.BlockSpec((rows, dh), lambda h, r: (r, h)),
            pl.BlockSpec((s, dh), lambda h, r: (0, heads + h)),
        ],
        out_specs=pl.BlockSpec((rows, nbp), lambda h, r: (r, h)),
        out_shape=jax.ShapeDtypeStruct((s, heads * nbp), BF16),
        scratch_shapes=[pltpu.VMEM((nbp, dh), F32)],
        compiler_params=_params(("arbitrary", "arbitrary")),
        name="moba_select",
    )(qkv, qkv)


def _moba_kernel(q_ref, b_ref, k_ref, v_ref, o_ref, kaug_ref, vext_ref, s_ref, *, scale, group):
    i = pl.program_id(1)
    blk, dh = q_ref.shape
    nb = k_ref.shape[0] // blk
    nbp = b_ref.shape[1]
    gk = group * blk
    c_exp = scale * LOG2E
    lane = lax.broadcasted_iota(jnp.int32, (blk, nbp), 1)

    @pl.when(i == 0)
    def _():
        def head_setup(n, carry):
            rows = pl.ds(pl.multiple_of(n * blk, blk), blk)
            kaug_ref[rows, :] = jnp.concatenate([k_ref[rows, :], (lane == n).astype(BF16)], axis=1)
            vext_ref[rows, :] = jnp.concatenate([v_ref[rows, :], (lane == 0).astype(BF16)], axis=1)
            return carry

        lax.fori_loop(0, nb, head_setup, 0)

    q_aug = jnp.concatenate([q_ref[...], b_ref[...]], axis=1)
    nt = (((1,), (1,)), ((), ()))

    own_row = pl.multiple_of(i * blk, blk)
    rest = (group - 1) * blk
    half = (group // 2) * blk
    k_own = jnp.concatenate([k_ref[pl.ds(own_row, blk), :], (lane == nb).astype(BF16)], axis=1)
    s_own = lax.dot_general(q_aug, k_own, nt, preferred_element_type=F32)
    row = lax.broadcasted_iota(jnp.int32, (blk, blk), 0)
    col = lax.broadcasted_iota(jnp.int32, (blk, blk), 1)
    s_ref[0, :, 0:blk] = jnp.where(col <= row, s_own, NEG_INF)
    s_ref[0, :, blk:half] = lax.dot_general(q_aug, kaug_ref[0:half - blk, :], nt, preferred_element_type=F32)
    s_ref[0, :, half:gk] = lax.dot_general(q_aug, kaug_ref[half - blk:rest, :], nt, preferred_element_type=F32)

    last_step = nb // group - 1

    def step(j, slot, m_prev, acc, look_ahead=True):
        if look_ahead:
            ahead = jnp.minimum(j + 1, last_step)
            rows_ahead = pl.ds(pl.multiple_of((ahead * group - 1) * blk, blk), gk)
            s_ref[1 - slot] = lax.dot_general(q_aug, kaug_ref[rows_ahead, :], nt, preferred_element_type=F32)
        sc = s_ref[slot]
        m_new = jnp.maximum(m_prev, jnp.max(sc, axis=1, keepdims=True))
        alpha = jnp.exp2((m_prev - m_new) * c_exp)
        p = jnp.exp2((sc - m_new) * c_exp).astype(BF16)
        head_row = pl.multiple_of(jnp.where(j == 0, own_row, (j * group - 1) * blk), blk)
        tail_row = pl.multiple_of(jnp.where(j == 0, 0, j * gk), blk)
        pv = (jnp.dot(p[:, 0:blk], vext_ref[pl.ds(head_row, blk), :], preferred_element_type=F32)
              + jnp.dot(p[:, blk:gk], vext_ref[pl.ds(tail_row, rest), :], preferred_element_type=F32))
        return m_new, alpha * acc + pv

    def body(t, carry):
        m_mid, acc_mid = step(2 * t, 0, *carry)
        return step(2 * t + 1, 1, m_mid, acc_mid)

    n_steps = i // group + 1
    init = (jnp.full((blk, 1), -jnp.inf, F32), jnp.zeros((blk, 2 * dh), F32))
    m_even, acc_even = lax.fori_loop(0, n_steps // 2, body, init)
    _, acc = lax.cond(n_steps % 2 == 1,
                      lambda: step(n_steps - 1, 0, m_even, acc_even, look_ahead=False),
                      lambda: (m_even, acc_even))
    o_ref[...] = (acc[:, :dh] / acc[:, dh:dh + 1]).astype(o_ref.dtype)


def _moba_attention(qkv, bias, heads, group):
    s, d3 = qkv.shape
    d = d3 // 3
    dh = d // heads
    nb = s // MOBA_BLOCK
    nbp = bias.shape[1] // heads
    assert nb % (2 * group) == 0 and group % 2 == 0 and dh == V7X_LANES and nb < nbp
    return pl.pallas_call(
        functools.partial(_moba_kernel, scale=dh ** -0.5, group=group),
        grid=(heads, nb),
        in_specs=[
            pl.BlockSpec((MOBA_BLOCK, dh), lambda h, i: (i, h)),
            pl.BlockSpec((MOBA_BLOCK, nbp), lambda h, i: (i, h)),
            pl.BlockSpec((s, dh), lambda h, i: (0, heads + h)),
            pl.BlockSpec((s, dh), lambda h, i: (0, 2 * heads + h)),
        ],
        out_specs=pl.BlockSpec((MOBA_BLOCK, dh), lambda h, i: (i, h)),
        out_shape=jax.ShapeDtypeStruct((s, d), BF16),
        scratch_shapes=[pltpu.VMEM((s, dh + nbp), BF16), pltpu.VMEM((s, 2 * dh), BF16),
                        pltpu.VMEM((2, MOBA_BLOCK, group * MOBA_BLOCK), F32)],
        compiler_params=_params(("arbitrary", "arbitrary")),
        name="moba_attention",
    )(qkv, bias, qkv, qkv)


def _rope_tables(s, dh):
    rope_dim = dh // 4
    inv = ROPE_THETA ** (-jnp.arange(0, rope_dim, 2, dtype=F32) / rope_dim)
    ang = jnp.arange(s).astype(F32)[:, None] * inv[None, :]
    cos, sin = jnp.cos(ang), jnp.sin(ang)
    pad = dh - rope_dim
    cos_t = jnp.concatenate([cos, cos, jnp.ones((s, pad), F32)], axis=1)
    sin_t = jnp.concatenate([-sin, sin, jnp.zeros((s, pad), F32)], axis=1)
    return cos_t, sin_t


def _moba_mixer(x, xn, w_qkv, w_out, group=8):
    s, d = x.shape
    dh = d // MOBA_HEADS
    tn = 512
    cos_t, sin_t = _rope_tables(s, dh)
    qkv = _matmul(xn, w_qkv, tm=1024, tn=tn, out_dtype=BF16, epilogue="rope", extra=(cos_t, sin_t),
                  rope_tiles=2 * d // tn, name="moba_qkv")
    att = _moba_attention(qkv, _moba_select(qkv, MOBA_HEADS), MOBA_HEADS, group)
    return _matmul(att, w_out, tm=1024, tn=512, out_dtype=F32, epilogue="residual", extra=(x,), name="moba_out")


def _mlstm_kernel(q_ref, k_ref, v_ref, og_ref, gc_ref, gr_ref, b_ref, o_ref, c_ref, m_ref):
    ci = pl.program_id(1)
    L, dk = q_ref.shape
    dv = v_ref.shape[1]

    @pl.when(ci == 0)
    def _():
        c_ref[...] = jnp.zeros(c_ref.shape, F32)
        m_ref[...] = jnp.zeros(m_ref.shape, F32)

    b = b_ref[0]
    b_i, b_f = b[:, 0:1], b[:, 1:2]
    gc = gc_ref[0]
    gr = gr_ref[0]
    li_col = gc[:, 0:1] + b_i
    lf_col = jax.nn.log_sigmoid(gc[:, 1:2] + b_f)
    li_row = gr[0:1, :] + b_i
    lf_row = jax.nn.log_sigmoid(gr[1:2, :] + b_f)

    t_idx = lax.broadcasted_iota(jnp.int32, (L, L), 0)
    s_idx = lax.broadcasted_iota(jnp.int32, (L, L), 1)
    causal = s_idx <= t_idx
    b_col = jnp.sum(jnp.where(causal, lf_row, 0.0), axis=1, keepdims=True)
    b_row = jnp.sum(jnp.where(t_idx <= s_idx, lf_col, 0.0), axis=0, keepdims=True)
    b_last = b_row[:, L - 1:L]
    m_prev = m_ref[0:1, 0:1]

    dmat = jnp.where(causal, b_col - b_row + li_row, NEG_INF)
    inter = b_col + m_prev
    m_t = jnp.maximum(inter, jnp.max(dmat, axis=1, keepdims=True))
    q = q_ref[...]
    k = k_ref[...]
    qk = lax.dot_general(q, k, (((1,), (1,)), ((), ())), preferred_element_type=F32) * (dk ** -0.5)
    a = (jnp.exp(dmat - m_t) * qk).astype(BF16)
    s_inter = jnp.exp(inter - m_t)

    ones_col = (lax.broadcasted_iota(jnp.int32, (L, V7X_LANES), 1) == 0).astype(BF16)
    v_ext = jnp.concatenate([v_ref[...], ones_col], axis=1)
    c_prev = c_ref[...]
    qc = jnp.dot(q, c_prev.astype(BF16), preferred_element_type=F32) * (dk ** -0.5)
    num_ext = s_inter * qc + jnp.dot(a, v_ext, preferred_element_type=F32)
    den = num_ext[:, dv:dv + 1]
    h = num_ext[:, :dv] / jnp.maximum(jnp.abs(den), jnp.exp(-m_t))
    o_ref[...] = (jax.nn.sigmoid(og_ref[...].astype(F32)) * h).astype(o_ref.dtype)

    g_col = b_last - b_col + li_col
    m_new = jnp.maximum(b_last + m_prev, jnp.max(g_col, axis=0, keepdims=True))
    decay = jnp.exp(b_last + m_prev - m_new)
    wk = jnp.exp(g_col - m_new)
    kw = (k.astype(F32) * wk).astype(BF16)
    upd = lax.dot_general(kw, v_ext, (((0,), (0,)), ((), ())), preferred_element_type=F32)
    c_ref[...] = decay * c_prev + upd
    m_ref[...] = jnp.broadcast_to(m_new, m_ref.shape)


def _mlstm_scan(proj, gates_col, gates_row, bias, heads, chunk):
    s = proj.shape[0]
    dv = proj.shape[1] // (3 * heads)
    dk = dv // 2
    nc = s // chunk
    return pl.pallas_call(
        _mlstm_kernel,
        grid=(heads, nc),
        in_specs=[
            pl.BlockSpec((chunk, dk), lambda h, c: (c, h)),
            pl.BlockSpec((chunk, dk), lambda h, c: (c, heads + h)),
            pl.BlockSpec((chunk, dv), lambda h, c: (c, heads + h)),
            pl.BlockSpec((chunk, dv), lambda h, c: (c, 2 * heads + h)),
            pl.BlockSpec((1, chunk, 2), lambda h, c: (h, c, 0)),
            pl.BlockSpec((1, 2, chunk), lambda h, c: (h, 0, c)),
            pl.BlockSpec((1, 1, 2), lambda h, c: (h, 0, 0)),
        ],
        out_specs=pl.BlockSpec((chunk, dv), lambda h, c: (c, h)),
        out_shape=jax.ShapeDtypeStruct((s, heads * dv), BF16),
        scratch_shapes=[pltpu.VMEM((dk, dv + V7X_LANES), F32), pltpu.VMEM((V7X_SUBLANES, V7X_LANES), F32)],
        compiler_params=_params(("arbitrary", "arbitrary")),
        name="mlstm_scan",
    )(proj, proj, proj, proj, gates_col, gates_row, bias)


def _mlstm_mixer(x, xn, w_in, b_i, b_f, w_out, chunk=256):
    h = MLSTM_HEADS
    n_main = w_in.shape[1] - 2 * h
    w_gates = jnp.pad(w_in[:, n_main:], ((0, 0), (0, V7X_LANES - 2 * h)))
    proj = _matmul(xn, w_in, tm=1024, tn=512, out_dtype=BF16, n=n_main, name="mlstm_in")
    gates = _matmul(xn, w_gates, tm=1024, tn=V7X_LANES, out_dtype=F32, name="mlstm_gates")
    pre = jnp.stack([gates[:, :h], gates[:, h:2 * h]], axis=0)
    gates_col = pre.transpose(2, 1, 0)
    gates_row = pre.transpose(2, 0, 1)
    bias = jnp.stack([b_i, b_f], axis=-1).reshape(h, 1, 2)
    hs = _mlstm_scan(proj, gates_col, gates_row, bias, h, chunk)
    return _matmul(hs, w_out, tm=1024, tn=512, out_dtype=F32, epilogue="residual", extra=(x,), name="mlstm_out")


FFN_TILE = 512


def _ffn_block(x, norm_g, w_up, conv_w, conv_b, w_down):
    f = w_down.shape[0]
    fpad = -f % FFN_TILE

    tab = jnp.concatenate([conv_w, conv_b[None, :], jnp.zeros((V7X_SUBLANES - CONV_W - 1, 2 * f), F32)], axis=0)
    tabs = [jnp.pad(tab[:, lo:lo + f], ((0, 0), (0, fpad))) for lo in (0, f)]
    return _conv_ffn(x, norm_g, _cast_pad(w_up, 1, 0, f, f + fpad), _cast_pad(w_up, 1, f, f, f + fpad), *tabs,
                     _cast_pad(w_down, 0, 0, f, f + fpad), tf=FFN_TILE)


def kernel(x, l0_norm_mix, l0_a_w_in, l0_a_ln_g, l0_a_ln_b, l0_a_w_s, l0_a_b_s, l0_a_w_out, l0_norm_ffn, l0_ffn_w_up, l0_ffn_conv_w, l0_ffn_conv_b, l0_ffn_w_down, l1_norm_mix, l1_b_w_qkv, l1_b_w_out, l1_norm_ffn, l1_ffn_w_up, l1_ffn_conv_w, l1_ffn_conv_b, l1_ffn_w_down, l2_norm_mix, l2_c_w_in, l2_c_b_i, l2_c_b_f, l2_c_w_out, l2_norm_ffn, l2_ffn_w_up, l2_ffn_conv_w, l2_ffn_conv_b, l2_ffn_w_down, l3_norm_mix, l3_a_w_in, l3_a_ln_g, l3_a_ln_b, l3_a_w_s, l3_a_b_s, l3_a_w_out, l3_norm_ffn, l3_ffn_w_up, l3_ffn_conv_w, l3_ffn_conv_b, l3_ffn_w_down, final_norm):
    batch, s, d = x.shape
    outs = []
    for bidx in range(batch):
        h = x[bidx]
        hn = _rmsnorm(h, l0_norm_mix, BF16)
        h = _gmlp_mixer(h, hn, l0_a_w_in, l0_a_ln_g, l0_a_ln_b, l0_a_w_s, l0_a_b_s, l0_a_w_out)
        h = _ffn_block(h, l0_norm_ffn, l0_ffn_w_up, l0_ffn_conv_w, l0_ffn_conv_b, l0_ffn_w_down)
        hn = _rmsnorm(h, l1_norm_mix, BF16)
        h = _moba_mixer(h, hn, l1_b_w_qkv, l1_b_w_out)
        h = _ffn_block(h, l1_norm_ffn, l1_ffn_w_up, l1_ffn_conv_w, l1_ffn_conv_b, l1_ffn_w_down)
        hn = _rmsnorm(h, l2_norm_mix, BF16)
        h = _mlstm_mixer(h, hn, l2_c_w_in, l2_c_b_i, l2_c_b_f, l2_c_w_out)
        h = _ffn_block(h, l2_norm_ffn, l2_ffn_w_up, l2_ffn_conv_w, l2_ffn_conv_b, l2_ffn_w_down)
        hn = _rmsnorm(h, l3_norm_mix, BF16)
        h = _gmlp_mixer(h, hn, l3_a_w_in, l3_a_ln_g, l3_a_ln_b, l3_a_w_s, l3_a_b_s, l3_a_w_out)
        h = _ffn_block(h, l3_norm_ffn, l3_ffn_w_up, l3_ffn_conv_w, l3_ffn_conv_b, l3_ffn_w_down)
        outs.append(_rmsnorm(h, final_norm, F32))
    return jnp.stack(outs, axis=0)
```

```python
import functools
import math

import jax
import jax.numpy as jnp
from jax import lax
from jax.experimental import pallas as pl
from jax.experimental.pallas import tpu as pltpu

F32 = jnp.float32
BF16 = jnp.bfloat16

V7X_LANES = 128
V7X_SUBLANES = 8
V7X_VMEM_BYTES = 64 * 1024 * 1024
VMEM_LIMIT_BYTES = V7X_VMEM_BYTES - 8 * 1024 * 1024

GMLP_CHUNK = 128
GMLP_GROUPS = 8
MOBA_HEADS = 32
MOBA_BLOCK = 256
MOBA_TOPK = 3
ROPE_THETA = 500000.0
MLSTM_HEADS = 8
CONV_W = 3
NEG_INF = -1e30
RMS_EPS = 1e-6
LN_EPS = 1e-5
LOG2E = math.log2(math.e)
NORM_ROWS = 64


def _params(semantics):
    return pltpu.CompilerParams(dimension_semantics=semantics, vmem_limit_bytes=VMEM_LIMIT_BYTES)


def _rmsnorm_kernel(x_ref, g_ref, o_ref):
    x = x_ref[...]
    y = x * lax.rsqrt(jnp.mean(x * x, axis=-1, keepdims=True) + RMS_EPS)
    o_ref[...] = (y * g_ref[...]).astype(o_ref.dtype)


def _rmsnorm(x, g, out_dtype, rows=256):
    s, d = x.shape
    return pl.pallas_call(
        _rmsnorm_kernel,
        grid=(s // rows,),
        in_specs=[pl.BlockSpec((rows, d), lambda i: (i, 0)), pl.BlockSpec((1, d), lambda i: (0, 0))],
        out_specs=pl.BlockSpec((rows, d), lambda i: (i, 0)),
        out_shape=jax.ShapeDtypeStruct((s, d), out_dtype),
        compiler_params=_params(("parallel",)),
        name="rmsnorm",
    )(x, g.reshape(1, d))


def _gelu(x):
    return 0.5 * x * (1.0 + lax.erf(x * math.sqrt(0.5)))


def _rope_heads(acc, cos, sin):
    rows, width = acc.shape
    lane = lax.broadcasted_iota(jnp.int32, (rows, V7X_LANES), 1)
    half = cos.shape[-1] // 8
    outs = []
    for h in range(width // V7X_LANES):
        xh = acc[:, h * V7X_LANES:(h + 1) * V7X_LANES]
        partner = jnp.where(lane < half, pltpu.roll(xh, V7X_LANES - half, axis=1), pltpu.roll(xh, half, axis=1))
        outs.append(xh * cos + partner * sin)
    return jnp.concatenate(outs, axis=1) if len(outs) > 1 else outs[0]


def _mm_kernel(*refs, epilogue, rope_tiles):
    x_ref, w_ref = refs[0], refs[1]
    o_ref = refs[-1]
    acc = jnp.dot(x_ref[...], w_ref[...].astype(BF16), preferred_element_type=F32)
    if epilogue == "cast":
        o_ref[...] = acc.astype(o_ref.dtype)
    elif epilogue == "gelu":
        o_ref[...] = _gelu(acc).astype(o_ref.dtype)
    elif epilogue == "residual":
        o_ref[...] = (refs[2][...] + acc).astype(o_ref.dtype)
    elif epilogue == "rope":
        cos_ref, sin_ref = refs[2], refs[3]
        j = pl.program_id(1)

        @pl.when(j < rope_tiles)
        def _():
            o_ref[...] = _rope_heads(acc, cos_ref[...], sin_ref[...]).astype(o_ref.dtype)

        @pl.when(j >= rope_tiles)
        def _():
            o_ref[...] = acc.astype(o_ref.dtype)
    else:
        raise ValueError(epilogue)


def _matmul(x, w, *, tm, tn, out_dtype, epilogue="cast", extra=(), rope_tiles=0, n=None, name="matmul"):
    m, k = x.shape
    n = n or w.shape[1]
    tm, tn = min(tm, m), min(tn, n)
    in_specs = [pl.BlockSpec((tm, k), lambda i, j: (i, 0)), pl.BlockSpec((k, tn), lambda i, j: (0, j))]
    if epilogue == "residual":
        in_specs.append(pl.BlockSpec((tm, tn), lambda i, j: (i, j)))
    elif epilogue == "rope":
        in_specs += [pl.BlockSpec((tm, V7X_LANES), lambda i, j: (i, 0))] * 2
    return pl.pallas_call(
        functools.partial(_mm_kernel, epilogue=epilogue, rope_tiles=rope_tiles),
        grid=(m // tm, n // tn),
        in_specs=in_specs,
        out_specs=pl.BlockSpec((tm, tn), lambda i, j: (i, j)),
        out_shape=jax.ShapeDtypeStruct((m, n), out_dtype),
        compiler_params=_params(("parallel", "parallel")),
        name=name,
    )(x, w, *extra)


def _cast_pad_kernel(w_ref, o_ref, *, valid_blocks):
    j = pl.program_id(0)

    @pl.when(j < valid_blocks)
    def _():
        o_ref[...] = w_ref[...].astype(o_ref.dtype)

    @pl.when(j >= valid_blocks)
    def _():
        o_ref[...] = jnp.zeros(o_ref.shape, o_ref.dtype)


def _cast_pad(w, axis, start, size, out_size, blk=256):
    assert start % blk == 0 and size % blk == 0 and out_size % blk == 0
    first, valid = start // blk, size // blk
    other = w.shape[1 - axis]
    if axis == 1:
        block = (other, blk)
        in_map = lambda j: (0, first + jnp.minimum(j, valid - 1))
        out_map = lambda j: (0, j)
        out_shape = (other, out_size)
    else:
        block = (blk, other)
        in_map = lambda j: (first + jnp.minimum(j, valid - 1), 0)
        out_map = lambda j: (j, 0)
        out_shape = (out_size, other)
    return pl.pallas_call(
        functools.partial(_cast_pad_kernel, valid_blocks=valid),
        grid=(out_size // blk,),
        in_specs=[pl.BlockSpec(block, in_map)],
        out_specs=pl.BlockSpec(block, out_map),
        out_shape=jax.ShapeDtypeStruct(out_shape, BF16),
        compiler_params=_params(("parallel",)),
        name="cast_pad",
    )(w)


def _ffn_kernel(x_ref, g_ref, og_ref, wg_ref, wu_ref, cg_ref, cu_ref, wd_ref, o_ref, xn_ref, carry_g, carry_u,
                *, sub, norm_out):
    i, j = pl.program_id(0), pl.program_id(1)
    tm = x_ref.shape[0]
    tf = wg_ref.shape[1]

    @pl.when(j == 0)
    def _():
        def norm_rows(c, carry):
            rows = pl.ds(pl.multiple_of(c * NORM_ROWS, NORM_ROWS), NORM_ROWS)
            x = x_ref[rows, :]
            y = x * lax.rsqrt(jnp.mean(x * x, axis=-1, keepdims=True) + RMS_EPS)
            xn_ref[rows, :] = (y * g_ref[...]).astype(BF16)
            o_ref[rows, :] = x
            return carry

        lax.fori_loop(0, tm // NORM_ROWS, norm_rows, 0)

    @pl.when(i == 0)
    def _():
        carry_g[j] = jnp.zeros(carry_g.shape[1:], F32)
        carry_u[j] = jnp.zeros(carry_u.shape[1:], F32)

    def conv(a, carry, c_ref, lanes):
        prev = carry[j, :, lanes]
        carry[j, :, lanes] = a[tm - V7X_SUBLANES:, :]
        ext = jnp.concatenate([prev, a], axis=0)
        a1 = ext[V7X_SUBLANES - 1:V7X_SUBLANES - 1 + tm, :]
        a2 = ext[V7X_SUBLANES - 2:V7X_SUBLANES - 2 + tm, :]
        c = c_ref[:, lanes]
        return c[3:4, :] + c[0:1, :] * a2 + c[1:2, :] * a1 + c[2:3, :] * a

    xn = xn_ref[...]
    pieces = [slice(k * sub, (k + 1) * sub) for k in range(tf // sub)]
    ups = [(jnp.dot(xn, wg_ref[:, p], preferred_element_type=F32), jnp.dot(xn, wu_ref[:, p], preferred_element_type=F32))
           for p in pieces]
    for p, (ag, au) in zip(pieces, ups):
        act = (_gelu(conv(ag, carry_g, cg_ref, p)) * conv(au, carry_u, cu_ref, p)).astype(BF16)
        o_ref[...] += jnp.dot(act, wd_ref[p, :], preferred_element_type=F32)

    if norm_out:
        @pl.when(j == pl.num_programs(1) - 1)
        def _():
            def norm_rows(c, carry):
                rows = pl.ds(pl.multiple_of(c * NORM_ROWS, NORM_ROWS), NORM_ROWS)
                y = o_ref[rows, :]
                y = y * lax.rsqrt(jnp.mean(y * y, axis=-1, keepdims=True) + RMS_EPS)
                o_ref[rows, :] = y * og_ref[...]
                return carry

            lax.fori_loop(0, tm // NORM_ROWS, norm_rows, 0)


def _conv_ffn(x, norm_g, out_g, w_gate, w_up, tab_gate, tab_up, w_down, *, tm=512, tf=512, sub=256):
    s, d = x.shape
    nf = w_down.shape[0] // tf
    return pl.pallas_call(
        functools.partial(_ffn_kernel, sub=sub, norm_out=out_g is not None),
        grid=(s // tm, nf),
        in_specs=[
            pl.BlockSpec((tm, d), lambda i, j: (i, 0), pipeline_mode=pl.Buffered(1)),
            pl.BlockSpec((1, d), lambda i, j: (0, 0)),
            pl.BlockSpec((1, d), lambda i, j: (0, 0)),
            pl.BlockSpec((d, tf), lambda i, j: (0, j)),
            pl.BlockSpec((d, tf), lambda i, j: (0, j)),
            pl.BlockSpec((V7X_SUBLANES, tf), lambda i, j: (0, j)),
            pl.BlockSpec((V7X_SUBLANES, tf), lambda i, j: (0, j)),
            pl.BlockSpec((tf, d), lambda i, j: (j, 0)),
        ],
        out_specs=pl.BlockSpec((tm, d), lambda i, j: (i, 0), pipeline_mode=pl.Buffered(1)),
        out_shape=jax.ShapeDtypeStruct((s, d), F32),
        scratch_shapes=[pltpu.VMEM((tm, d), BF16), pltpu.VMEM((nf, V7X_SUBLANES, tf), F32),
                        pltpu.VMEM((nf, V7X_SUBLANES, tf), F32)],
        compiler_params=_params(("arbitrary", "arbitrary")),
        name="conv_ffn",
    )(x, norm_g.reshape(1, d), (norm_g if out_g is None else out_g).reshape(1, d), w_gate, w_up, tab_gate, tab_up, w_down)


def _gmlp_gate_kernel(z_ref, lng_ref, lnb_ref, ws_ref, bs_ref, o_ref):
    rows = z_ref.shape[0]
    d = o_ref.shape[1]
    gd = d // GMLP_GROUPS
    v = z_ref[:, d:].astype(F32)
    mu = jnp.mean(v, axis=-1, keepdims=True)
    vc = v - mu
    var = jnp.mean(vc * vc, axis=-1, keepdims=True)
    vln = ((vc * lax.rsqrt(var + LN_EPS)) * lng_ref[...] + lnb_ref[...]).astype(BF16)
    t_idx = lax.broadcasted_iota(jnp.int32, (GMLP_CHUNK, GMLP_CHUNK), 0)
    s_idx = lax.broadcasted_iota(jnp.int32, (GMLP_CHUNK, GMLP_CHUNK), 1)
    bs = bs_ref[...]
    for g in range(GMLP_GROUPS):
        w = jnp.where(s_idx <= t_idx, ws_ref[g], 0.0).astype(BF16)
        bias = bs[:, g:g + 1]
        for c in range(rows // GMLP_CHUNK):
            r0 = c * GMLP_CHUNK
            sv = jnp.dot(w, vln[r0:r0 + GMLP_CHUNK, g * gd:(g + 1) * gd], preferred_element_type=F32) + bias
            u = z_ref[r0:r0 + GMLP_CHUNK, g * gd:(g + 1) * gd].astype(F32)
            o_ref[r0:r0 + GMLP_CHUNK, g * gd:(g + 1) * gd] = (u * sv).astype(o_ref.dtype)


def _gmlp_gate(z, ln_g, ln_b, w_s, b_s, rows=256):
    s, d2 = z.shape
    d = d2 // 2
    return pl.pallas_call(
        _gmlp_gate_kernel,
        grid=(s // rows,),
        in_specs=[
            pl.BlockSpec((rows, d2), lambda i: (i, 0)),
            pl.BlockSpec((1, d), lambda i: (0, 0)),
            pl.BlockSpec((1, d), lambda i: (0, 0)),
            pl.BlockSpec(w_s.shape, lambda i: (0, 0, 0)),
            pl.BlockSpec((GMLP_CHUNK, GMLP_GROUPS), lambda i: (0, 0)),
        ],
        out_specs=pl.BlockSpec((rows, d), lambda i: (i, 0)),
        out_shape=jax.ShapeDtypeStruct((s, d), BF16),
        compiler_params=_params(("parallel",)),
        name="gmlp_gate",
    )(z, ln_g.reshape(1, d), ln_b.reshape(1, d), w_s, b_s.T)


def _gmlp_mixer(x, xn, w_in, ln_g, ln_b, w_s, b_s, w_out):
    z = _matmul(xn, w_in, tm=1024, tn=512, out_dtype=BF16, epilogue="gelu", name="gmlp_in")
    y = _gmlp_gate(z, ln_g, ln_b, w_s, b_s)
    return _matmul(y, w_out, tm=1024, tn=512, out_dtype=F32, epilogue="residual", extra=(x,), name="gmlp_out")


def _moba_select_kernel(q_ref, k_ref, o_ref, kmean_ref):
    r = pl.program_id(1)
    rows, _ = q_ref.shape
    blk = MOBA_BLOCK
    nb = k_ref.shape[0] // blk
    nbp = kmean_ref.shape[0]

    @pl.when(r == 0)
    def _():
        kmean_ref[...] = jnp.zeros(kmean_ref.shape, F32)

        def mean_body(n, carry):
            kb = k_ref[pl.ds(pl.multiple_of(n * blk, blk), blk), :]
            kmean_ref[pl.ds(n, 1), :] = jnp.mean(kb.astype(F32), axis=0, keepdims=True)
            return carry

        lax.fori_loop(0, nb, mean_body, 0)

    n_idx = lax.broadcasted_iota(jnp.int32, (nbp, rows), 0)
    own = (r * rows + lax.broadcasted_iota(jnp.int32, (nbp, rows), 1)) // blk
    gate = lax.dot_general(kmean_ref[...].astype(BF16), q_ref[...], (((1,), (1,)), ((), ())),
                           preferred_element_type=F32)
    valid = n_idx < own
    g = jnp.where(valid, gate, NEG_INF)
    sel = jnp.zeros((nbp, rows), jnp.bool_)
    for _ in range(MOBA_TOPK):
        m = jnp.max(g, axis=0, keepdims=True)
        idx = jnp.min(jnp.where(g == m, n_idx, nbp), axis=0, keepdims=True)
        hit = n_idx == idx
        sel = jnp.logical_or(sel, hit)
        g = jnp.where(hit, -jnp.inf, g)
    keep = jnp.logical_or(jnp.logical_and(sel, valid), n_idx == nb)
    o_ref[...] = jnp.where(keep, 0.0, NEG_INF).T.astype(o_ref.dtype)


def _moba_select(qkv, heads, rows=1024):
    s, d3 = qkv.shape
    dh = d3 // (3 * heads)
    nb = s // MOBA_BLOCK
    nbp = -(-(nb + 1) // V7X_LANES) * V7X_LANES
    rows = min(rows, s)
    return pl.pallas_call(
        _moba_select_kernel,
        grid=(heads, s // rows),
        in_specs=[
            pl.BlockSpec((rows, dh), lambda h, r: (r, h)),
            pl.BlockSpec((s, dh), lambda h, r: (0, heads + h)),
        ],
        out_specs=pl.BlockSpec((rows, nbp), lambda h, r: (r, h)),
        out_shape=jax.ShapeDtypeStruct((s, heads * nbp), BF16),
        scratch_shapes=[pltpu.VMEM((nbp, dh), F32)],
        compiler_params=_params(("arbitrary", "arbitrary")),
        name="moba_select",
    )(qkv, qkv)


def _moba_kernel(q_ref, b_ref, k_ref, v_ref, o_ref, kaug_ref, vext_ref, s_ref, *, scale, group):
    i = pl.program_id(1)
    blk, dh = q_ref.shape
    nb = k_ref.shape[0] // blk
    nbp = b_ref.shape[1]
    gk = group * blk
    c_exp = scale * LOG2E
    lane = lax.broadcasted_iota(jnp.int32, (blk, nbp), 1)

    @pl.when(i == 0)
    def _():
        def head_setup(n, carry):
            rows = pl.ds(pl.multiple_of(n * blk, blk), blk)
            kaug_ref[rows, :] = jnp.concatenate([k_ref[rows, :], (lane == n).astype(BF16)], axis=1)
            vext_ref[rows, :] = jnp.concatenate([v_ref[rows, :], (lane == 0).astype(BF16)], axis=1)
            return carry

        lax.fori_loop(0, nb, head_setup, 0)

    q_aug = jnp.concatenate([q_ref[...], b_ref[...]], axis=1)
    nt = (((1,), (1,)), ((), ()))

    own_row = pl.multiple_of(i * blk, blk)
    rest = (group - 1) * blk
    half = (group // 2) * blk
    k_own = jnp.concatenate([k_ref[pl.ds(own_row, blk), :], (lane == nb).astype(BF16)], axis=1)
    s_own = lax.dot_general(q_aug, k_own, nt, preferred_element_type=F32)
    row = lax.broadcasted_iota(jnp.int32, (blk, blk), 0)
    col = lax.broadcasted_iota(jnp.int32, (blk, blk), 1)
    s_ref[0, :, 0:blk] = jnp.where(col <= row, s_own, NEG_INF)
    s_ref[0, :, blk:half] = lax.dot_general(q_aug, kaug_ref[0:half - blk, :], nt, preferred_element_type=F32)
    s_ref[0, :, half:gk] = lax.dot_general(q_aug, kaug_ref[half - blk:rest, :], nt, preferred_element_type=F32)

    last_step = nb // group - 1

    def step(j, slot, m_prev, acc, look_ahead=True):
        if look_ahead:
            ahead = jnp.minimum(j + 1, last_step)
            rows_ahead = pl.ds(pl.multiple_of((ahead * group - 1) * blk, blk), gk)
            s_ref[1 - slot] = lax.dot_general(q_aug, kaug_ref[rows_ahead, :], nt, preferred_element_type=F32)
        sc = s_ref[slot]
        m_new = jnp.maximum(m_prev, jnp.max(sc, axis=1, keepdims=True))
        alpha = jnp.exp2((m_prev - m_new) * c_exp)
        p = jnp.exp2((sc - m_new) * c_exp).astype(BF16)
        head_row = pl.multiple_of(jnp.where(j == 0, own_row, (j * group - 1) * blk), blk)
        tail_row = pl.multiple_of(jnp.where(j == 0, 0, j * gk), blk)
        pv = (jnp.dot(p[:, 0:blk], vext_ref[pl.ds(head_row, blk), :], preferred_element_type=F32)
              + jnp.dot(p[:, blk:gk], vext_ref[pl.ds(tail_row, rest), :], preferred_element_type=F32))
        return m_new, alpha * acc + pv

    def steps(first, count, carry, last_looks_ahead=True):
        for k in range(count):
            carry = step(first + k, k % 2, *carry, look_ahead=last_looks_ahead or k < count - 1)
        return carry

    n_steps = i // group + 1
    n_quads = n_steps // 4
    init = (jnp.full((blk, 1), -jnp.inf, F32), jnp.zeros((blk, 2 * dh), F32))
    carry = lax.fori_loop(0, n_quads, lambda t, c: steps(4 * t, 4, c), init)
    carry = lax.cond(n_steps % 4 >= 2, lambda: steps(4 * n_quads, 2, carry), lambda: carry)
    _, acc = lax.cond(n_steps % 2 == 1,
                      lambda: steps(n_steps - 1, 1, carry, last_looks_ahead=False),
                      lambda: carry)
    o_ref[...] = (acc[:, :dh] / acc[:, dh:dh + 1]).astype(o_ref.dtype)


def _moba_attention(qkv, bias, heads, group):
    s, d3 = qkv.shape
    d = d3 // 3
    dh = d // heads
    nb = s // MOBA_BLOCK
    nbp = bias.shape[1] // heads
    assert nb % (2 * group) == 0 and group % 2 == 0 and dh == V7X_LANES and nb < nbp
    return pl.pallas_call(
        functools.partial(_moba_kernel, scale=dh ** -0.5, group=group),
        grid=(heads, nb),
        in_specs=[
            pl.BlockSpec((MOBA_BLOCK, dh), lambda h, i: (i, h)),
            pl.BlockSpec((MOBA_BLOCK, nbp), lambda h, i: (i, h)),
            pl.BlockSpec((s, dh), lambda h, i: (0, heads + h)),
            pl.BlockSpec((s, dh), lambda h, i: (0, 2 * heads + h)),
        ],
        out_specs=pl.BlockSpec((MOBA_BLOCK, dh), lambda h, i: (i, h)),
        out_shape=jax.ShapeDtypeStruct((s, d), BF16),
        scratch_shapes=[pltpu.VMEM((s, dh + nbp), BF16), pltpu.VMEM((s, 2 * dh), BF16),
                        pltpu.VMEM((2, MOBA_BLOCK, group * MOBA_BLOCK), F32)],
        compiler_params=_params(("arbitrary", "arbitrary")),
        name="moba_attention",
    )(qkv, bias, qkv, qkv)


def _rope_tables(s, dh):
    rope_dim = dh // 4
    inv = ROPE_THETA ** (-jnp.arange(0, rope_dim, 2, dtype=F32) / rope_dim)
    ang = jnp.arange(s).astype(F32)[:, None] * inv[None, :]
    cos, sin = jnp.cos(ang), jnp.sin(ang)
    pad = dh - rope_dim
    cos_t = jnp.concatenate([cos, cos, jnp.ones((s, pad), F32)], axis=1)
    sin_t = jnp.concatenate([-sin, sin, jnp.zeros((s, pad), F32)], axis=1)
    return cos_t, sin_t


def _moba_mixer(x, xn, w_qkv, w_out, group=8):
    s, d = x.shape
    dh = d // MOBA_HEADS
    tn = 512
    cos_t, sin_t = _rope_tables(s, dh)
    qkv = _matmul(xn, w_qkv, tm=1024, tn=tn, out_dtype=BF16, epilogue="rope", extra=(cos_t, sin_t),
                  rope_tiles=2 * d // tn, name="moba_qkv")
    att = _moba_attention(qkv, _moba_select(qkv, MOBA_HEADS), MOBA_HEADS, group)
    return _matmul(att, w_out, tm=1024, tn=512, out_dtype=F32, epilogue="residual", extra=(x,), name="moba_out")


def _mlstm_kernel(q_ref, k_ref, v_ref, og_ref, gc_ref, gr_ref, b_ref, o_ref, c_ref, m_ref):
    ci = pl.program_id(1)
    L, dk = q_ref.shape
    dv = v_ref.shape[1]

    @pl.when(ci == 0)
    def _():
        c_ref[...] = jnp.zeros(c_ref.shape, F32)
        m_ref[...] = jnp.zeros(m_ref.shape, F32)

    b = b_ref[0]
    b_i, b_f = b[:, 0:1], b[:, 1:2]
    gc = gc_ref[0]
    gr = gr_ref[0]
    li_col = gc[:, 0:1] + b_i
    lf_col = jax.nn.log_sigmoid(gc[:, 1:2] + b_f)
    li_row = gr[0:1, :] + b_i
    lf_row = jax.nn.log_sigmoid(gr[1:2, :] + b_f)

    t_idx = lax.broadcasted_iota(jnp.int32, (L, L), 0)
    s_idx = lax.broadcasted_iota(jnp.int32, (L, L), 1)
    causal = s_idx <= t_idx
    b_col = jnp.sum(jnp.where(causal, lf_row, 0.0), axis=1, keepdims=True)
    b_row = jnp.sum(jnp.where(t_idx <= s_idx, lf_col, 0.0), axis=0, keepdims=True)
    b_last = b_row[:, L - 1:L]
    m_prev = m_ref[0:1, 0:1]

    dmat = jnp.where(causal, b_col - b_row + li_row, NEG_INF)
    inter = b_col + m_prev
    m_t = jnp.maximum(inter, jnp.max(dmat, axis=1, keepdims=True))
    q = q_ref[...]
    k = k_ref[...]
    qk = lax.dot_general(q, k, (((1,), (1,)), ((), ())), preferred_element_type=F32) * (dk ** -0.5)
    a = (jnp.exp(dmat - m_t) * qk).astype(BF16)
    s_inter = jnp.exp(inter - m_t)

    ones_col = (lax.broadcasted_iota(jnp.int32, (L, V7X_LANES), 1) == 0).astype(BF16)
    v_ext = jnp.concatenate([v_ref[...], ones_col], axis=1)
    c_prev = c_ref[...]
    qc = jnp.dot(q, c_prev.astype(BF16), preferred_element_type=F32) * (dk ** -0.5)
    num_ext = s_inter * qc + jnp.dot(a, v_ext, preferred_element_type=F32)
    den = num_ext[:, dv:dv + 1]
    h = num_ext[:, :dv] / jnp.maximum(jnp.abs(den), jnp.exp(-m_t))
    o_ref[...] = (jax.nn.sigmoid(og_ref[...].astype(F32)) * h).astype(o_ref.dtype)

    g_col = b_last - b_col + li_col
    m_new = jnp.maximum(b_last + m_prev, jnp.max(g_col, axis=0, keepdims=True))
    decay = jnp.exp(b_last + m_prev - m_new)
    wk = jnp.exp(g_col - m_new)
    kw = (k.astype(F32) * wk).astype(BF16)
    upd = lax.dot_general(kw, v_ext, (((0,), (0,)), ((), ())), preferred_element_type=F32)
    c_ref[...] = decay * c_prev + upd
    m_ref[...] = jnp.broadcast_to(m_new, m_ref.shape)


def _mlstm_scan(proj, gates_col, gates_row, bias, heads, chunk):
    s = proj.shape[0]
    dv = proj.shape[1] // (3 * heads)
    dk = dv // 2
    nc = s // chunk
    return pl.pallas_call(
        _mlstm_kernel,
        grid=(heads, nc),
        in_specs=[
            pl.BlockSpec((chunk, dk), lambda h, c: (c, h)),
            pl.BlockSpec((chunk, dk), lambda h, c: (c, heads + h)),
            pl.BlockSpec((chunk, dv), lambda h, c: (c, heads + h)),
            pl.BlockSpec((chunk, dv), lambda h, c: (c, 2 * heads + h)),
            pl.BlockSpec((1, chunk, 2), lambda h, c: (h, c, 0)),
            pl.BlockSpec((1, 2, chunk), lambda h, c: (h, 0, c)),
            pl.BlockSpec((1, 1, 2), lambda h, c: (h, 0, 0)),
        ],
        out_specs=pl.BlockSpec((chunk, dv), lambda h, c: (c, h)),
        out_shape=jax.ShapeDtypeStruct((s, heads * dv), BF16),
        scratch_shapes=[pltpu.VMEM((dk, dv + V7X_LANES), F32), pltpu.VMEM((V7X_SUBLANES, V7X_LANES), F32)],
        compiler_params=_params(("arbitrary", "arbitrary")),
        name="mlstm_scan",
    )(proj, proj, proj, proj, gates_col, gates_row, bias)


def _mlstm_mixer(x, xn, w_in, b_i, b_f, w_out, chunk=256):
    h = MLSTM_HEADS
    n_main = w_in.shape[1] - 2 * h
    w_gates = jnp.pad(w_in[:, n_main:], ((0, 0), (0, V7X_LANES - 2 * h)))
    proj = _matmul(xn, w_in, tm=1024, tn=512, out_dtype=BF16, n=n_main, name="mlstm_in")
    gates = _matmul(xn, w_gates, tm=1024, tn=V7X_LANES, out_dtype=F32, name="mlstm_gates")
    pre = jnp.stack([gates[:, :h], gates[:, h:2 * h]], axis=0)
    gates_col = pre.transpose(2, 1, 0)
    gates_row = pre.transpose(2, 0, 1)
    bias = jnp.stack([b_i, b_f], axis=-1).reshape(h, 1, 2)
    hs = _mlstm_scan(proj, gates_col, gates_row, bias, h, chunk)
    return _matmul(hs, w_out, tm=1024, tn=512, out_dtype=F32, epilogue="residual", extra=(x,), name="mlstm_out")


FFN_TILE = 512


def _ffn_block(x, norm_g, w_up, conv_w, conv_b, w_down, out_g=None):
    f = w_down.shape[0]
    fpad = -f % FFN_TILE

    tab = jnp.concatenate([conv_w, conv_b[None, :], jnp.zeros((V7X_SUBLANES - CONV_W - 1, 2 * f), F32)], axis=0)
    tabs = [jnp.pad(tab[:, lo:lo + f], ((0, 0), (0, fpad))) for lo in (0, f)]
    return _conv_ffn(x, norm_g, out_g, _cast_pad(w_up, 1, 0, f, f + fpad), _cast_pad(w_up, 1, f, f, f + fpad), *tabs,
                     _cast_pad(w_down, 0, 0, f, f + fpad), tf=FFN_TILE)


def kernel(x, l0_norm_mix, l0_a_w_in, l0_a_ln_g, l0_a_ln_b, l0_a_w_s, l0_a_b_s, l0_a_w_out, l0_norm_ffn, l0_ffn_w_up, l0_ffn_conv_w, l0_ffn_conv_b, l0_ffn_w_down, l1_norm_mix, l1_b_w_qkv, l1_b_w_out, l1_norm_ffn, l1_ffn_w_up, l1_ffn_conv_w, l1_ffn_conv_b, l1_ffn_w_down, l2_norm_mix, l2_c_w_in, l2_c_b_i, l2_c_b_f, l2_c_w_out, l2_norm_ffn, l2_ffn_w_up, l2_ffn_conv_w, l2_ffn_conv_b, l2_ffn_w_down, l3_norm_mix, l3_a_w_in, l3_a_ln_g, l3_a_ln_b, l3_a_w_s, l3_a_b_s, l3_a_w_out, l3_norm_ffn, l3_ffn_w_up, l3_ffn_conv_w, l3_ffn_conv_b, l3_ffn_w_down, final_norm):
    batch, s, d = x.shape
    outs = []
    for bidx in range(batch):
        h = x.reshape(s, d) if batch == 1 else x[bidx]
        hn = _rmsnorm(h, l0_norm_mix, BF16)
        h = _gmlp_mixer(h, hn, l0_a_w_in, l0_a_ln_g, l0_a_ln_b, l0_a_w_s, l0_a_b_s, l0_a_w_out)
        h = _ffn_block(h, l0_norm_ffn, l0_ffn_w_up, l0_ffn_conv_w, l0_ffn_conv_b, l0_ffn_w_down)
        hn = _rmsnorm(h, l1_norm_mix, BF16)
        h = _moba_mixer(h, hn, l1_b_w_qkv, l1_b_w_out)
        h = _ffn_block(h, l1_norm_ffn, l1_ffn_w_up, l1_ffn_conv_w, l1_ffn_conv_b, l1_ffn_w_down)
        hn = _rmsnorm(h, l2_norm_mix, BF16)
        h = _mlstm_mixer(h, hn, l2_c_w_in, l2_c_b_i, l2_c_b_f, l2_c_w_out)
        h = _ffn_block(h, l2_norm_ffn, l2_ffn_w_up, l2_ffn_conv_w, l2_ffn_conv_b, l2_ffn_w_down)
        hn = _rmsnorm(h, l3_norm_mix, BF16)
        h = _gmlp_mixer(h, hn, l3_a_w_in, l3_a_ln_g, l3_a_ln_b, l3_a_w_s, l3_a_b_s, l3_a_w_out)
        outs.append(_ffn_block(h, l3_norm_ffn, l3_ffn_w_up, l3_ffn_conv_w, l3_ffn_conv_b, l3_ffn_w_down,
                               out_g=final_norm))
    return outs[0].reshape(1, s, d) if batch == 1 else jnp.stack(outs, axis=0)
```

```python
import functools
import math

import jax
import jax.numpy as jnp
from jax import lax
from jax.experimental import pallas as pl
from jax.experimental.pallas import tpu as pltpu

F32 = jnp.float32
BF16 = jnp.bfloat16

V7X_LANES = 128
V7X_SUBLANES = 8
V7X_VMEM_BYTES = 64 * 1024 * 1024
VMEM_LIMIT_BYTES = V7X_VMEM_BYTES - 8 * 1024 * 1024
FFN_VMEM_LIMIT_BYTES = V7X_VMEM_BYTES - 2 * 1024 * 1024

GMLP_CHUNK = 128
GMLP_GROUPS = 8
MOBA_HEADS = 32
MOBA_BLOCK = 256
MOBA_TOPK = 3
ROPE_THETA = 500000.0
MLSTM_HEADS = 8
CONV_W = 3
NEG_INF = -1e30
RMS_EPS = 1e-6
LN_EPS = 1e-5
LOG2E = math.log2(math.e)
NORM_ROWS = 32


def _params(semantics, vmem_limit_bytes=VMEM_LIMIT_BYTES):
    return pltpu.CompilerParams(dimension_semantics=semantics, vmem_limit_bytes=vmem_limit_bytes)


def _rmsnorm_kernel(x_ref, g_ref, o_ref):
    x = x_ref[...]
    y = x * lax.rsqrt(jnp.mean(x * x, axis=-1, keepdims=True) + RMS_EPS)
    o_ref[...] = (y * g_ref[...]).astype(o_ref.dtype)


def _rmsnorm(x, g, out_dtype, rows=256):
    s, d = x.shape
    return pl.pallas_call(
        _rmsnorm_kernel,
        grid=(s // rows,),
        in_specs=[pl.BlockSpec((rows, d), lambda i: (i, 0)), pl.BlockSpec((1, d), lambda i: (0, 0))],
        out_specs=pl.BlockSpec((rows, d), lambda i: (i, 0)),
        out_shape=jax.ShapeDtypeStruct((s, d), out_dtype),
        compiler_params=_params(("parallel",)),
        name="rmsnorm",
    )(x, g.reshape(1, d))


def _gelu(x):
    return 0.5 * x * (1.0 + lax.erf(x * math.sqrt(0.5)))


def _rope_heads(acc, cos, sin):
    rows, width = acc.shape
    lane = lax.broadcasted_iota(jnp.int32, (rows, V7X_LANES), 1)
    half = cos.shape[-1] // 8
    outs = []
    for h in range(width // V7X_LANES):
        xh = acc[:, h * V7X_LANES:(h + 1) * V7X_LANES]
        partner = jnp.where(lane < half, pltpu.roll(xh, V7X_LANES - half, axis=1), pltpu.roll(xh, half, axis=1))
        outs.append(xh * cos + partner * sin)
    return jnp.concatenate(outs, axis=1) if len(outs) > 1 else outs[0]


def _mm_kernel(*refs, epilogue, rope_tiles):
    x_ref, w_ref = refs[0], refs[1]
    o_ref = refs[-1]
    acc = jnp.dot(x_ref[...], w_ref[...].astype(BF16), preferred_element_type=F32)
    if epilogue == "cast":
        o_ref[...] = acc.astype(o_ref.dtype)
    elif epilogue == "gelu":
        o_ref[...] = _gelu(acc).astype(o_ref.dtype)
    elif epilogue == "residual":
        o_ref[...] = (refs[2][...] + acc).astype(o_ref.dtype)
    elif epilogue == "rope":
        cos_ref, sin_ref = refs[2], refs[3]
        j = pl.program_id(1)

        @pl.when(j < rope_tiles)
        def _():
            o_ref[...] = _rope_heads(acc, cos_ref[...], sin_ref[...]).astype(o_ref.dtype)

        @pl.when(j >= rope_tiles)
        def _():
            o_ref[...] = acc.astype(o_ref.dtype)
    else:
        raise ValueError(epilogue)


def _matmul(x, w, *, tm, tn, out_dtype, epilogue="cast", extra=(), rope_tiles=0, n=None, name="matmul"):
    m, k = x.shape
    n = n or w.shape[1]
    tm, tn = min(tm, m), min(tn, n)
    in_specs = [pl.BlockSpec((tm, k), lambda i, j: (i, 0)), pl.BlockSpec((k, tn), lambda i, j: (0, j))]
    if epilogue == "residual":
        in_specs.append(pl.BlockSpec((tm, tn), lambda i, j: (i, j)))
    elif epilogue == "rope":
        in_specs += [pl.BlockSpec((tm, V7X_LANES), lambda i, j: (i, 0))] * 2
    return pl.pallas_call(
        functools.partial(_mm_kernel, epilogue=epilogue, rope_tiles=rope_tiles),
        grid=(m // tm, n // tn),
        in_specs=in_specs,
        out_specs=pl.BlockSpec((tm, tn), lambda i, j: (i, j)),
        out_shape=jax.ShapeDtypeStruct((m, n), out_dtype),
        compiler_params=_params(("parallel", "parallel")),
        name=name,
    )(x, w, *extra)


def _cast_pad_kernel(w_ref, o_ref, *, valid_blocks):
    j = pl.program_id(0)

    @pl.when(j < valid_blocks)
    def _():
        o_ref[...] = w_ref[...].astype(o_ref.dtype)

    @pl.when(j >= valid_blocks)
    def _():
        o_ref[...] = jnp.zeros(o_ref.shape, o_ref.dtype)


def _cast_pad(w, axis, start, size, out_size, blk=256):
    assert start % blk == 0 and size % blk == 0 and out_size % blk == 0
    first, valid = start // blk, size // blk
    other = w.shape[1 - axis]
    if axis == 1:
        block = (other, blk)
        in_map = lambda j: (0, first + jnp.minimum(j, valid - 1))
        out_map = lambda j: (0, j)
        out_shape = (other, out_size)
    else:
        block = (blk, other)
        in_map = lambda j: (first + jnp.minimum(j, valid - 1), 0)
        out_map = lambda j: (j, 0)
        out_shape = (out_size, other)
    return pl.pallas_call(
        functools.partial(_cast_pad_kernel, valid_blocks=valid),
        grid=(out_size // blk,),
        in_specs=[pl.BlockSpec(block, in_map)],
        out_specs=pl.BlockSpec(block, out_map),
        out_shape=jax.ShapeDtypeStruct(out_shape, BF16),
        compiler_params=_params(("parallel",)),
        name="cast_pad",
    )(w)


def _ffn_kernel(x_hbm, g_ref, og_ref, wg_ref, wu_ref, cg_ref, cu_ref, wd_ref, o_ref, xn_ref, xbuf, xsem,
                carry_g, carry_u, *, sub, norm_out):
    i, j = pl.program_id(0), pl.program_id(1)
    tm = o_ref.shape[0]
    tf = wg_ref.shape[1]
    n_chunks = tm // NORM_ROWS

    def x_copy(c, slot):
        rows = pl.ds(pl.multiple_of(i * tm + c * NORM_ROWS, NORM_ROWS), NORM_ROWS)
        return pltpu.make_async_copy(x_hbm.at[rows, :], xbuf.at[slot], xsem.at[slot])

    @pl.when(j == 0)
    def _():
        x_copy(0, 0).start()

        def norm_rows(c, carry):
            slot = c % 2
            x_copy(c, slot).wait()

            @pl.when(c + 1 < n_chunks)
            def _():
                x_copy(c + 1, 1 - slot).start()

            rows = pl.ds(pl.multiple_of(c * NORM_ROWS, NORM_ROWS), NORM_ROWS)
            x = xbuf[slot]
            y = x * lax.rsqrt(jnp.mean(x * x, axis=-1, keepdims=True) + RMS_EPS)
            xn_ref[rows, :] = (y * g_ref[...]).astype(BF16)
            o_ref[rows, :] = x
            return carry

        lax.fori_loop(0, n_chunks, norm_rows, 0)

    @pl.when(i == 0)
    def _():
        carry_g[j] = jnp.zeros(carry_g.shape[1:], F32)
        carry_u[j] = jnp.zeros(carry_u.shape[1:], F32)

    def conv(a, carry, c_ref, lanes):
        prev = carry[j, :, lanes]
        carry[j, :, lanes] = a[tm - V7X_SUBLANES:, :]
        ext = jnp.concatenate([prev, a], axis=0)
        a1 = ext[V7X_SUBLANES - 1:V7X_SUBLANES - 1 + tm, :]
        a2 = ext[V7X_SUBLANES - 2:V7X_SUBLANES - 2 + tm, :]
        c = c_ref[:, lanes]
        return c[3:4, :] + c[0:1, :] * a2 + c[1:2, :] * a1 + c[2:3, :] * a

    xn = xn_ref[...]
    pieces = [slice(k * sub, (k + 1) * sub) for k in range(tf // sub)]
    ups = [(jnp.dot(xn, wg_ref[:, p], preferred_element_type=F32), jnp.dot(xn, wu_ref[:, p], preferred_element_type=F32))
           for p in pieces]
    for p, (ag, au) in zip(pieces, ups):
        act = (_gelu(conv(ag, carry_g, cg_ref, p)) * conv(au, carry_u, cu_ref, p)).astype(BF16)
        o_ref[...] += jnp.dot(act, wd_ref[p, :], preferred_element_type=F32)

    if norm_out:
        @pl.when(j == pl.num_programs(1) - 1)
        def _():
            def norm_rows(c, carry):
                rows = pl.ds(pl.multiple_of(c * NORM_ROWS, NORM_ROWS), NORM_ROWS)
                y = o_ref[rows, :]
                y = y * lax.rsqrt(jnp.mean(y * y, axis=-1, keepdims=True) + RMS_EPS)
                o_ref[rows, :] = y * og_ref[...]
                return carry

            lax.fori_loop(0, tm // NORM_ROWS, norm_rows, 0)


def _conv_ffn(x, norm_g, out_g, w_gate, w_up, tab_gate, tab_up, w_down, *, tm=1024, tf=512, sub=256):
    s, d = x.shape
    tm = min(tm, s)
    nf = w_down.shape[0] // tf
    return pl.pallas_call(
        functools.partial(_ffn_kernel, sub=sub, norm_out=out_g is not None),
        grid=(s // tm, nf),
        in_specs=[
            pl.BlockSpec(memory_space=pl.ANY),
            pl.BlockSpec((1, d), lambda i, j: (0, 0)),
            pl.BlockSpec((1, d), lambda i, j: (0, 0)),
            pl.BlockSpec((d, tf), lambda i, j: (0, j)),
            pl.BlockSpec((d, tf), lambda i, j: (0, j)),
            pl.BlockSpec((V7X_SUBLANES, tf), lambda i, j: (0, j)),
            pl.BlockSpec((V7X_SUBLANES, tf), lambda i, j: (0, j)),
            pl.BlockSpec((tf, d), lambda i, j: (j, 0)),
        ],
        out_specs=pl.BlockSpec((tm, d), lambda i, j: (i, 0), pipeline_mode=pl.Buffered(1)),
        out_shape=jax.ShapeDtypeStruct((s, d), F32),
        scratch_shapes=[pltpu.VMEM((tm, d), BF16), pltpu.VMEM((2, NORM_ROWS, d), F32), pltpu.SemaphoreType.DMA((2,)),
                        pltpu.VMEM((nf, V7X_SUBLANES, tf), F32), pltpu.VMEM((nf, V7X_SUBLANES, tf), F32)],
        compiler_params=_params(("arbitrary", "arbitrary"), FFN_VMEM_LIMIT_BYTES),
        name="conv_ffn",
    )(x, norm_g.reshape(1, d), (norm_g if out_g is None else out_g).reshape(1, d), w_gate, w_up, tab_gate, tab_up, w_down)


def _gmlp_gate_kernel(z_ref, lng_ref, lnb_ref, ws_ref, bs_ref, o_ref):
    rows = z_ref.shape[0]
    d = o_ref.shape[1]
    gd = d // GMLP_GROUPS
    v = z_ref[:, d:].astype(F32)
    mu = jnp.mean(v, axis=-1, keepdims=True)
    vc = v - mu
    var = jnp.mean(vc * vc, axis=-1, keepdims=True)
    vln = ((vc * lax.rsqrt(var + LN_EPS)) * lng_ref[...] + lnb_ref[...]).astype(BF16)
    t_idx = lax.broadcasted_iota(jnp.int32, (GMLP_CHUNK, GMLP_CHUNK), 0)
    s_idx = lax.broadcasted_iota(jnp.int32, (GMLP_CHUNK, GMLP_CHUNK), 1)
    bs = bs_ref[...]
    for g in range(GMLP_GROUPS):
        w = jnp.where(s_idx <= t_idx, ws_ref[g], 0.0).astype(BF16)
        bias = bs[:, g:g + 1]
        for c in range(rows // GMLP_CHUNK):
            r0 = c * GMLP_CHUNK
            sv = jnp.dot(w, vln[r0:r0 + GMLP_CHUNK, g * gd:(g + 1) * gd], preferred_element_type=F32) + bias
            u = z_ref[r0:r0 + GMLP_CHUNK, g * gd:(g + 1) * gd].astype(F32)
            o_ref[r0:r0 + GMLP_CHUNK, g * gd:(g + 1) * gd] = (u * sv).astype(o_ref.dtype)


def _gmlp_gate(z, ln_g, ln_b, w_s, b_s, rows=256):
    s, d2 = z.shape
    d = d2 // 2
    return pl.pallas_call(
        _gmlp_gate_kernel,
        grid=(s // rows,),
        in_specs=[
            pl.BlockSpec((rows, d2), lambda i: (i, 0)),
            pl.BlockSpec((1, d), lambda i: (0, 0)),
            pl.BlockSpec((1, d), lambda i: (0, 0)),
            pl.BlockSpec(w_s.shape, lambda i: (0, 0, 0)),
            pl.BlockSpec((GMLP_CHUNK, GMLP_GROUPS), lambda i: (0, 0)),
        ],
        out_specs=pl.BlockSpec((rows, d), lambda i: (i, 0)),
        out_shape=jax.ShapeDtypeStruct((s, d), BF16),
        compiler_params=_params(("parallel",)),
        name="gmlp_gate",
    )(z, ln_g.reshape(1, d), ln_b.reshape(1, d), w_s, b_s.T)


def _gmlp_mixer(x, xn, w_in, ln_g, ln_b, w_s, b_s, w_out):
    z = _matmul(xn, w_in, tm=1024, tn=512, out_dtype=BF16, epilogue="gelu", name="gmlp_in")
    y = _gmlp_gate(z, ln_g, ln_b, w_s, b_s)
    return _matmul(y, w_out, tm=1024, tn=512, out_dtype=F32, epilogue="residual", extra=(x,), name="gmlp_out")


def _moba_select_kernel(q_ref, k_ref, o_ref, kmean_ref):
    r = pl.program_id(1)
    rows, _ = q_ref.shape
    blk = MOBA_BLOCK
    nb = k_ref.shape[0] // blk
    nbp = kmean_ref.shape[0]

    @pl.when(r == 0)
    def _():
        kmean_ref[...] = jnp.zeros(kmean_ref.shape, F32)

        def mean_body(n, carry):
            kb = k_ref[pl.ds(pl.multiple_of(n * blk, blk), blk), :]
            kmean_ref[pl.ds(n, 1), :] = jnp.mean(kb.astype(F32), axis=0, keepdims=True)
            return carry

        lax.fori_loop(0, nb, mean_body, 0)

    n_idx = lax.broadcasted_iota(jnp.int32, (nbp, rows), 0)
    own = (r * rows + lax.broadcasted_iota(jnp.int32, (nbp, rows), 1)) // blk
    gate = lax.dot_general(kmean_ref[...].astype(BF16), q_ref[...], (((1,), (1,)), ((), ())),
                           preferred_element_type=F32)
    valid = n_idx < own
    g = jnp.where(valid, gate, NEG_INF)
    sel = jnp.zeros((nbp, rows), jnp.bool_)
    for _ in range(MOBA_TOPK):
        m = jnp.max(g, axis=0, keepdims=True)
        idx = jnp.min(jnp.where(g == m, n_idx, nbp), axis=0, keepdims=True)
        hit = n_idx == idx
        sel = jnp.logical_or(sel, hit)
        g = jnp.where(hit, -jnp.inf, g)
    keep = jnp.logical_or(jnp.logical_and(sel, valid), n_idx == nb)
    o_ref[...] = jnp.where(keep, 0.0, NEG_INF).T.astype(o_ref.dtype)


def _moba_select(qkv, heads, rows=1024):
    s, d3 = qkv.shape
    dh = d3 // (3 * heads)
    nb = s // MOBA_BLOCK
    nbp = -(-(nb + 1) // V7X_LANES) * V7X_LANES
    rows = min(rows, s)
    return pl.pallas_call(
        _moba_select_kernel,
        grid=(heads, s // rows),
        in_specs=[
            pl.BlockSpec((rows, dh), lambda h, r: (r, h)),
            pl.BlockSpec((s, dh), lambda h, r: (0, heads + h)),
        ],
        out_specs=pl.BlockSpec((rows, nbp), lambda h, r: (r, h)),
        out_shape=jax.ShapeDtypeStruct((s, heads * nbp), BF16),
        scratch_shapes=[pltpu.VMEM((nbp, dh), F32)],
        compiler_params=_params(("arbitrary", "arbitrary")),
        name="moba_select",
    )(qkv, qkv)


def _moba_kernel(q_ref, b_ref, k_ref, v_ref, o_ref, kaug_ref, vext_ref, s_ref, *, scale, group):
    i = pl.program_id(1)
    blk, dh = q_ref.shape
    nb = k_ref.shape[0] // blk
    nbp = b_ref.shape[1]
    gk = group * blk
    c_exp = scale * LOG2E
    lane = lax.broadcasted_iota(jnp.int32, (blk, nbp), 1)

    @pl.when(i == 0)
    def _():
        def head_setup(n, carry):
            rows = pl.ds(pl.multiple_of(n * blk, blk), blk)
            kaug_ref[rows, :] = jnp.concatenate([k_ref[rows, :], (lane == n).astype(BF16)], axis=1)
            vext_ref[rows, :] = jnp.concatenate([v_ref[rows, :], (lane == 0).astype(BF16)], axis=1)
            return carry

        lax.fori_loop(0, nb, head_setup, 0)

    q_aug = jnp.concatenate([q_ref[...], b_ref[...]], axis=1)
    nt = (((1,), (1,)), ((), ()))

    own_row = pl.multiple_of(i * blk, blk)
    rest = (group - 1) * blk
    half = (group // 2) * blk
    k_own = jnp.concatenate([k_ref[pl.ds(own_row, blk), :], (lane == nb).astype(BF16)], axis=1)
    s_own = lax.dot_general(q_aug, k_own, nt, preferred_element_type=F32)
    row = lax.broadcasted_iota(jnp.int32, (blk, blk), 0)
    col = lax.broadcasted_iota(jnp.int32, (blk, blk), 1)
    s_ref[0, :, 0:blk] = jnp.where(col <= row, s_own, NEG_INF)
    s_ref[0, :, blk:half] = lax.dot_general(q_aug, kaug_ref[0:half - blk, :], nt, preferred_element_type=F32)
    s_ref[0, :, half:gk] = lax.dot_general(q_aug, kaug_ref[half - blk:rest, :], nt, preferred_element_type=F32)

    last_step = nb // group - 1

    def step(j, slot, m_prev, acc, look_ahead=True):
        if look_ahead:
            ahead = jnp.minimum(j + 1, last_step)
            rows_ahead = pl.ds(pl.multiple_of((ahead * group - 1) * blk, blk), gk)
            s_ref[1 - slot] = lax.dot_general(q_aug, kaug_ref[rows_ahead, :], nt, preferred_element_type=F32)
        sc = s_ref[slot]
        m_new = jnp.maximum(m_prev, jnp.max(sc, axis=1, keepdims=True))
        alpha = jnp.exp2((m_prev - m_new) * c_exp)
        p = jnp.exp2((sc - m_new) * c_exp).astype(BF16)
        head_row = pl.multiple_of(jnp.where(j == 0, own_row, (j * group - 1) * blk), blk)
        tail_row = pl.multiple_of(jnp.where(j == 0, 0, j * gk), blk)
        pv = (jnp.dot(p[:, 0:blk], vext_ref[pl.ds(head_row, blk), :], preferred_element_type=F32)
              + jnp.dot(p[:, blk:gk], vext_ref[pl.ds(tail_row, rest), :], preferred_element_type=F32))
        return m_new, alpha * acc + pv

    def steps(first, count, carry, last_looks_ahead=True):
        for k in range(count):
            carry = step(first + k, k % 2, *carry, look_ahead=last_looks_ahead or k < count - 1)
        return carry

    n_steps = i // group + 1
    n_quads = n_steps // 4
    init = (jnp.full((blk, 1), -jnp.inf, F32), jnp.zeros((blk, 2 * dh), F32))
    carry = lax.fori_loop(0, n_quads, lambda t, c: steps(4 * t, 4, c), init)
    carry = lax.cond(n_steps % 4 >= 2, lambda: steps(4 * n_quads, 2, carry), lambda: carry)
    _, acc = lax.cond(n_steps % 2 == 1,
                      lambda: steps(n_steps - 1, 1, carry, last_looks_ahead=False),
                      lambda: carry)
    o_ref[...] = (acc[:, :dh] / acc[:, dh:dh + 1]).astype(o_ref.dtype)


def _moba_attention(qkv, bias, heads, group):
    s, d3 = qkv.shape
    d = d3 // 3
    dh = d // heads
    nb = s // MOBA_BLOCK
    nbp = bias.shape[1] // heads
    assert nb % (2 * group) == 0 and group % 2 == 0 and dh == V7X_LANES and nb < nbp
    return pl.pallas_call(
        functools.partial(_moba_kernel, scale=dh ** -0.5, group=group),
        grid=(heads, nb),
        in_specs=[
            pl.BlockSpec((MOBA_BLOCK, dh), lambda h, i: (i, h)),
            pl.BlockSpec((MOBA_BLOCK, nbp), lambda h, i: (i, h)),
            pl.BlockSpec((s, dh), lambda h, i: (0, heads + h)),
            pl.BlockSpec((s, dh), lambda h, i: (0, 2 * heads + h)),
        ],
        out_specs=pl.BlockSpec((MOBA_BLOCK, dh), lambda h, i: (i, h)),
        out_shape=jax.ShapeDtypeStruct((s, d), BF16),
        scratch_shapes=[pltpu.VMEM((s, dh + nbp), BF16), pltpu.VMEM((s, 2 * dh), BF16),
                        pltpu.VMEM((2, MOBA_BLOCK, group * MOBA_BLOCK), F32)],
        compiler_params=_params(("arbitrary", "arbitrary")),
        name="moba_attention",
    )(qkv, bias, qkv, qkv)


def _rope_tables(s, dh):
    rope_dim = dh // 4
    inv = ROPE_THETA ** (-jnp.arange(0, rope_dim, 2, dtype=F32) / rope_dim)
    ang = jnp.arange(s).astype(F32)[:, None] * inv[None, :]
    cos, sin = jnp.cos(ang), jnp.sin(ang)
    pad = dh - rope_dim
    cos_t = jnp.concatenate([cos, cos, jnp.ones((s, pad), F32)], axis=1)
    sin_t = jnp.concatenate([-sin, sin, jnp.zeros((s, pad), F32)], axis=1)
    return cos_t, sin_t


def _moba_mixer(x, xn, w_qkv, w_out, group=8):
    s, d = x.shape
    dh = d // MOBA_HEADS
    tn = 512
    cos_t, sin_t = _rope_tables(s, dh)
    qkv = _matmul(xn, w_qkv, tm=1024, tn=tn, out_dtype=BF16, epilogue="rope", extra=(cos_t, sin_t),
                  rope_tiles=2 * d // tn, name="moba_qkv")
    att = _moba_attention(qkv, _moba_select(qkv, MOBA_HEADS), MOBA_HEADS, group)
    return _matmul(att, w_out, tm=1024, tn=512, out_dtype=F32, epilogue="residual", extra=(x,), name="moba_out")


def _mlstm_kernel(q_ref, k_ref, v_ref, og_ref, gc_ref, gr_ref, b_ref, o_ref, c_ref, m_ref):
    ci = pl.program_id(1)
    L, dk = q_ref.shape
    dv = v_ref.shape[1]

    @pl.when(ci == 0)
    def _():
        c_ref[...] = jnp.zeros(c_ref.shape, F32)
        m_ref[...] = jnp.zeros(m_ref.shape, F32)

    b = b_ref[0]
    b_i, b_f = b[:, 0:1], b[:, 1:2]
    gc = gc_ref[0]
    gr = gr_ref[0]
    li_col = gc[:, 0:1] + b_i
    lf_col = jax.nn.log_sigmoid(gc[:, 1:2] + b_f)
    li_row = gr[0:1, :] + b_i
    lf_row = jax.nn.log_sigmoid(gr[1:2, :] + b_f)

    t_idx = lax.broadcasted_iota(jnp.int32, (L, L), 0)
    s_idx = lax.broadcasted_iota(jnp.int32, (L, L), 1)
    causal = s_idx <= t_idx
    b_col = jnp.sum(jnp.where(causal, lf_row, 0.0), axis=1, keepdims=True)
    b_row = jnp.sum(jnp.where(t_idx <= s_idx, lf_col, 0.0), axis=0, keepdims=True)
    b_last = b_row[:, L - 1:L]
    m_prev = m_ref[0:1, 0:1]

    dmat = jnp.where(causal, b_col - b_row + li_row, NEG_INF)
    inter = b_col + m_prev
    m_t = jnp.maximum(inter, jnp.max(dmat, axis=1, keepdims=True))
    q = q_ref[...]
    k = k_ref[...]
    qk = lax.dot_general(q, k, (((1,), (1,)), ((), ())), preferred_element_type=F32) * (dk ** -0.5)
    a = (jnp.exp(dmat - m_t) * qk).astype(BF16)
    s_inter = jnp.exp(inter - m_t)

    ones_col = (lax.broadcasted_iota(jnp.int32, (L, V7X_LANES), 1) == 0).astype(BF16)
    v_ext = jnp.concatenate([v_ref[...], ones_col], axis=1)
    c_prev = c_ref[...]
    qc = jnp.dot(q, c_prev.astype(BF16), preferred_element_type=F32) * (dk ** -0.5)
    num_ext = s_inter * qc + jnp.dot(a, v_ext, preferred_element_type=F32)
    den = num_ext[:, dv:dv + 1]
    h = num_ext[:, :dv] / jnp.maximum(jnp.abs(den), jnp.exp(-m_t))
    o_ref[...] = (jax.nn.sigmoid(og_ref[...].astype(F32)) * h).astype(o_ref.dtype)

    g_col = b_last - b_col + li_col
    m_new = jnp.maximum(b_last + m_prev, jnp.max(g_col, axis=0, keepdims=True))
    decay = jnp.exp(b_last + m_prev - m_new)
    wk = jnp.exp(g_col - m_new)
    kw = (k.astype(F32) * wk).astype(BF16)
    upd = lax.dot_general(kw, v_ext, (((0,), (0,)), ((), ())), preferred_element_type=F32)
    c_ref[...] = decay * c_prev + upd
    m_ref[...] = jnp.broadcast_to(m_new, m_ref.shape)


def _mlstm_scan(proj, gates_col, gates_row, bias, heads, chunk):
    s = proj.shape[0]
    dv = proj.shape[1] // (3 * heads)
    dk = dv // 2
    nc = s // chunk
    return pl.pallas_call(
        _mlstm_kernel,
        grid=(heads, nc),
        in_specs=[
            pl.BlockSpec((chunk, dk), lambda h, c: (c, h)),
            pl.BlockSpec((chunk, dk), lambda h, c: (c, heads + h)),
            pl.BlockSpec((chunk, dv), lambda h, c: (c, heads + h)),
            pl.BlockSpec((chunk, dv), lambda h, c: (c, 2 * heads + h)),
            pl.BlockSpec((1, chunk, 2), lambda h, c: (h, c, 0)),
            pl.BlockSpec((1, 2, chunk), lambda h, c: (h, 0, c)),
            pl.BlockSpec((1, 1, 2), lambda h, c: (h, 0, 0)),
        ],
        out_specs=pl.BlockSpec((chunk, dv), lambda h, c: (c, h)),
        out_shape=jax.ShapeDtypeStruct((s, heads * dv), BF16),
        scratch_shapes=[pltpu.VMEM((dk, dv + V7X_LANES), F32), pltpu.VMEM((V7X_SUBLANES, V7X_LANES), F32)],
        compiler_params=_params(("arbitrary", "arbitrary")),
        name="mlstm_scan",
    )(proj, proj, proj, proj, gates_col, gates_row, bias)


def _mlstm_mixer(x, xn, w_in, b_i, b_f, w_out, chunk=256):
    h = MLSTM_HEADS
    n_main = w_in.shape[1] - 2 * h
    w_gates = jnp.pad(w_in[:, n_main:], ((0, 0), (0, V7X_LANES - 2 * h)))
    proj = _matmul(xn, w_in, tm=1024, tn=512, out_dtype=BF16, n=n_main, name="mlstm_in")
    gates = _matmul(xn, w_gates, tm=1024, tn=V7X_LANES, out_dtype=F32, name="mlstm_gates")
    pre = jnp.stack([gates[:, :h], gates[:, h:2 * h]], axis=0)
    gates_col = pre.transpose(2, 1, 0)
    gates_row = pre.transpose(2, 0, 1)
    bias = jnp.stack([b_i, b_f], axis=-1).reshape(h, 1, 2)
    hs = _mlstm_scan(proj, gates_col, gates_row, bias, h, chunk)
    return _matmul(hs, w_out, tm=1024, tn=512, out_dtype=F32, epilogue="residual", extra=(x,), name="mlstm_out")


FFN_TILE = 512


def _ffn_block(x, norm_g, w_up, conv_w, conv_b, w_down, out_g=None):
    f = w_down.shape[0]
    fpad = -f % FFN_TILE

    tab = jnp.concatenate([conv_w, conv_b[None, :], jnp.zeros((V7X_SUBLANES - CONV_W - 1, 2 * f), F32)], axis=0)
    tabs = [jnp.pad(tab[:, lo:lo + f], ((0, 0), (0, fpad))) for lo in (0, f)]
    return _conv_ffn(x, norm_g, out_g, _cast_pad(w_up, 1, 0, f, f + fpad), _cast_pad(w_up, 1, f, f, f + fpad), *tabs,
                     _cast_pad(w_down, 0, 0, f, f + fpad), tf=FFN_TILE)


def kernel(x, l0_norm_mix, l0_a_w_in, l0_a_ln_g, l0_a_ln_b, l0_a_w_s, l0_a_b_s, l0_a_w_out, l0_norm_ffn, l0_ffn_w_up, l0_ffn_conv_w, l0_ffn_conv_b, l0_ffn_w_down, l1_norm_mix, l1_b_w_qkv, l1_b_w_out, l1_norm_ffn, l1_ffn_w_up, l1_ffn_conv_w, l1_ffn_conv_b, l1_ffn_w_down, l2_norm_mix, l2_c_w_in, l2_c_b_i, l2_c_b_f, l2_c_w_out, l2_norm_ffn, l2_ffn_w_up, l2_ffn_conv_w, l2_ffn_conv_b, l2_ffn_w_down, l3_norm_mix, l3_a_w_in, l3_a_ln_g, l3_a_ln_b, l3_a_w_s, l3_a_b_s, l3_a_w_out, l3_norm_ffn, l3_ffn_w_up, l3_ffn_conv_w, l3_ffn_conv_b, l3_ffn_w_down, final_norm):
    batch, s, d = x.shape
    outs = []
    for bidx in range(batch):
        h = x.reshape(s, d) if batch == 1 else x[bidx]
        hn = _rmsnorm(h, l0_norm_mix, BF16)
        h = _gmlp_mixer(h, hn, l0_a_w_in, l0_a_ln_g, l0_a_ln_b, l0_a_w_s, l0_a_b_s, l0_a_w_out)
        h = _ffn_block(h, l0_norm_ffn, l0_ffn_w_up, l0_ffn_conv_w, l0_ffn_conv_b, l0_ffn_w_down)
        hn = _rmsnorm(h, l1_norm_mix, BF16)
        h = _moba_mixer(h, hn, l1_b_w_qkv, l1_b_w_out)
        h = _ffn_block(h, l1_norm_ffn, l1_ffn_w_up, l1_ffn_conv_w, l1_ffn_conv_b, l1_ffn_w_down)
        hn = _rmsnorm(h, l2_norm_mix, BF16)
        h = _mlstm_mixer(h, hn, l2_c_w_in, l2_c_b_i, l2_c_b_f, l2_c_w_out)
        h = _ffn_block(h, l2_norm_ffn, l2_ffn_w_up, l2_ffn_conv_w, l2_ffn_conv_b, l2_ffn_w_down)
        hn = _rmsnorm(h, l3_norm_mix, BF16)
        h = _gmlp_mixer(h, hn, l3_a_w_in, l3_a_ln_g, l3_a_ln_b, l3_a_w_s, l3_a_b_s, l3_a_w_out)
        outs.append(_ffn_block(h, l3_norm_ffn, l3_ffn_w_up, l3_ffn_conv_w, l3_ffn_conv_b, l3_ffn_w_down,
                               out_g=final_norm))
    return outs[0].reshape(1, s, d) if batch == 1 else jnp.stack(outs, axis=0)
```

```python
import functools
import math

import jax
import jax.numpy as jnp
from jax import lax
from jax.experimental import pallas as pl
from jax.experimental.pallas import tpu as pltpu

F32 = jnp.float32
BF16 = jnp.bfloat16

V7X_LANES = 128
V7X_SUBLANES = 8
V7X_VMEM_BYTES = 64 * 1024 * 1024
VMEM_LIMIT_BYTES = V7X_VMEM_BYTES - 8 * 1024 * 1024
FFN_VMEM_LIMIT_BYTES = V7X_VMEM_BYTES - 2 * 1024 * 1024

GMLP_CHUNK = 128
GMLP_GROUPS = 8
MOBA_HEADS = 32
MOBA_BLOCK = 256
MOBA_TOPK = 3
ROPE_THETA = 500000.0
ROPE_FRACTION = 4
MLSTM_HEADS = 8
CONV_W = 3
NEG_INF = -1e30
RMS_EPS = 1e-6
LN_EPS = 1e-5
LOG2E = math.log2(math.e)
NORM_ROWS = 32

PROJ_ROWS, PROJ_COLS = 1024, 512
FFN_ROWS, FFN_TILE, FFN_PIECE = 1024, 512, 256
STREAM_ROWS = 512
SELECT_ROWS = 1024
MOBA_GROUP = 8
MLSTM_CHUNK = 256
CAST_BLOCK = 256


def _params(semantics, vmem_limit_bytes=VMEM_LIMIT_BYTES):
    return pltpu.CompilerParams(dimension_semantics=semantics, vmem_limit_bytes=vmem_limit_bytes)


def _rmsnorm_kernel(x_ref, g_ref, o_ref):
    x = x_ref[...]
    y = x * lax.rsqrt(jnp.mean(x * x, axis=-1, keepdims=True) + RMS_EPS)
    o_ref[...] = (y * g_ref[...]).astype(o_ref.dtype)


def _rmsnorm(x, g, out_dtype, rows=STREAM_ROWS):
    s, d = x.shape
    return pl.pallas_call(
        _rmsnorm_kernel,
        grid=(s // rows,),
        in_specs=[pl.BlockSpec((rows, d), lambda i: (i, 0)), pl.BlockSpec((1, d), lambda i: (0, 0))],
        out_specs=pl.BlockSpec((rows, d), lambda i: (i, 0)),
        out_shape=jax.ShapeDtypeStruct((s, d), out_dtype),
        compiler_params=_params(("parallel",)),
        name="rmsnorm",
    )(x, g.reshape(1, d))


def _gelu(x):
    return 0.5 * x * (1.0 + lax.erf(x * math.sqrt(0.5)))


def _rope_heads(acc, cos, sin):
    rows, width = acc.shape
    lane = lax.broadcasted_iota(jnp.int32, (rows, V7X_LANES), 1)
    half = cos.shape[-1] // (2 * ROPE_FRACTION)
    outs = []
    for h in range(width // V7X_LANES):
        xh = acc[:, h * V7X_LANES:(h + 1) * V7X_LANES]
        partner = jnp.where(lane < half, pltpu.roll(xh, V7X_LANES - half, axis=1), pltpu.roll(xh, half, axis=1))
        outs.append(xh * cos + partner * sin)
    return jnp.concatenate(outs, axis=1) if len(outs) > 1 else outs[0]


def _mm_kernel(*refs, epilogue, rope_tiles):
    x_ref, w_ref = refs[0], refs[1]
    o_ref = refs[-1]
    acc = jnp.dot(x_ref[...], w_ref[...].astype(BF16), preferred_element_type=F32)
    if epilogue == "cast":
        o_ref[...] = acc.astype(o_ref.dtype)
    elif epilogue == "gelu":
        o_ref[...] = _gelu(acc).astype(o_ref.dtype)
    elif epilogue == "residual":
        o_ref[...] = (refs[2][...] + acc).astype(o_ref.dtype)
    elif epilogue == "rope":
        cos_ref, sin_ref = refs[2], refs[3]
        j = pl.program_id(1)

        @pl.when(j < rope_tiles)
        def _():
            o_ref[...] = _rope_heads(acc, cos_ref[...], sin_ref[...]).astype(o_ref.dtype)

        @pl.when(j >= rope_tiles)
        def _():
            o_ref[...] = acc.astype(o_ref.dtype)
    else:
        raise ValueError(epilogue)


def _matmul(x, w, *, out_dtype, tm=PROJ_ROWS, tn=PROJ_COLS, epilogue="cast", extra=(), rope_tiles=0, n=None, name="matmul"):
    m, k = x.shape
    n = n or w.shape[1]
    tm, tn = min(tm, m), min(tn, n)
    in_specs = [pl.BlockSpec((tm, k), lambda i, j: (i, 0)), pl.BlockSpec((k, tn), lambda i, j: (0, j))]
    if epilogue == "residual":
        in_specs.append(pl.BlockSpec((tm, tn), lambda i, j: (i, j)))
    elif epilogue == "rope":
        in_specs += [pl.BlockSpec((tm, V7X_LANES), lambda i, j: (i, 0))] * 2
    return pl.pallas_call(
        functools.partial(_mm_kernel, epilogue=epilogue, rope_tiles=rope_tiles),
        grid=(m // tm, n // tn),
        in_specs=in_specs,
        out_specs=pl.BlockSpec((tm, tn), lambda i, j: (i, j)),
        out_shape=jax.ShapeDtypeStruct((m, n), out_dtype),
        compiler_params=_params(("parallel", "parallel")),
        name=name,
    )(x, w, *extra)


def _cast_pad_kernel(w_ref, o_ref, *, valid_blocks):
    j = pl.program_id(0)

    @pl.when(j < valid_blocks)
    def _():
        o_ref[...] = w_ref[...].astype(o_ref.dtype)

    @pl.when(j >= valid_blocks)
    def _():
        o_ref[...] = jnp.zeros(o_ref.shape, o_ref.dtype)


def _cast_pad(w, axis, start, size, out_size, blk=CAST_BLOCK):
    assert start % blk == 0 and size % blk == 0 and out_size % blk == 0
    first, valid = start // blk, size // blk
    other = w.shape[1 - axis]
    if axis == 1:
        block = (other, blk)
        in_map = lambda j: (0, first + jnp.minimum(j, valid - 1))
        out_map = lambda j: (0, j)
        out_shape = (other, out_size)
    else:
        block = (blk, other)
        in_map = lambda j: (first + jnp.minimum(j, valid - 1), 0)
        out_map = lambda j: (j, 0)
        out_shape = (out_size, other)
    return pl.pallas_call(
        functools.partial(_cast_pad_kernel, valid_blocks=valid),
        grid=(out_size // blk,),
        in_specs=[pl.BlockSpec(block, in_map)],
        out_specs=pl.BlockSpec(block, out_map),
        out_shape=jax.ShapeDtypeStruct(out_shape, BF16),
        compiler_params=_params(("parallel",)),
        name="cast_pad",
    )(w)


def _ffn_kernel(x_hbm, g_ref, og_ref, wg_ref, wu_ref, cg_ref, cu_ref, wd_ref, o_ref, xn_ref, xbuf, xsem,
                carry_g, carry_u, *, sub, norm_out):
    i, j = pl.program_id(0), pl.program_id(1)
    tm = o_ref.shape[0]
    tf = wg_ref.shape[1]
    n_chunks = tm // NORM_ROWS

    def x_copy(c, slot):
        rows = pl.ds(pl.multiple_of(i * tm + c * NORM_ROWS, NORM_ROWS), NORM_ROWS)
        return pltpu.make_async_copy(x_hbm.at[rows, :], xbuf.at[slot], xsem.at[slot])

    @pl.when(j == 0)
    def _():
        x_copy(0, 0).start()

        def norm_rows(c, carry):
            slot = c % 2
            x_copy(c, slot).wait()

            @pl.when(c + 1 < n_chunks)
            def _():
                x_copy(c + 1, 1 - slot).start()

            rows = pl.ds(pl.multiple_of(c * NORM_ROWS, NORM_ROWS), NORM_ROWS)
            x = xbuf[slot]
            y = x * lax.rsqrt(jnp.mean(x * x, axis=-1, keepdims=True) + RMS_EPS)
            xn_ref[rows, :] = (y * g_ref[...]).astype(BF16)
            o_ref[rows, :] = x
            return carry

        lax.fori_loop(0, n_chunks, norm_rows, 0)

    @pl.when(i == 0)
    def _():
        carry_g[j] = jnp.zeros(carry_g.shape[1:], F32)
        carry_u[j] = jnp.zeros(carry_u.shape[1:], F32)

    def conv(a, carry, c_ref, lanes):
        prev = carry[j, :, lanes]
        carry[j, :, lanes] = a[tm - V7X_SUBLANES:, :]
        ext = jnp.concatenate([prev, a], axis=0)
        a1 = ext[V7X_SUBLANES - 1:V7X_SUBLANES - 1 + tm, :]
        a2 = ext[V7X_SUBLANES - 2:V7X_SUBLANES - 2 + tm, :]
        c = c_ref[:, lanes]
        return c[3:4, :] + c[0:1, :] * a2 + c[1:2, :] * a1 + c[2:3, :] * a

    xn = xn_ref[...]
    pieces = [slice(k * sub, (k + 1) * sub) for k in range(tf // sub)]
    ups = [(jnp.dot(xn, wg_ref[:, p], preferred_element_type=F32), jnp.dot(xn, wu_ref[:, p], preferred_element_type=F32))
           for p in pieces]
    for p, (ag, au) in zip(pieces, ups):
        act = (_gelu(conv(ag, carry_g, cg_ref, p)) * conv(au, carry_u, cu_ref, p)).astype(BF16)
        o_ref[...] += jnp.dot(act, wd_ref[p, :], preferred_element_type=F32)

    if norm_out:
        @pl.when(j == pl.num_programs(1) - 1)
        def _():
            def norm_rows(c, carry):
                rows = pl.ds(pl.multiple_of(c * NORM_ROWS, NORM_ROWS), NORM_ROWS)
                y = o_ref[rows, :]
                y = y * lax.rsqrt(jnp.mean(y * y, axis=-1, keepdims=True) + RMS_EPS)
                o_ref[rows, :] = y * og_ref[...]
                return carry

            lax.fori_loop(0, tm // NORM_ROWS, norm_rows, 0)


def _conv_ffn(x, norm_g, out_g, w_gate, w_up, tab_gate, tab_up, w_down, *, tm=FFN_ROWS, tf=FFN_TILE, sub=FFN_PIECE):
    s, d = x.shape
    tm = min(tm, s)
    nf = w_down.shape[0] // tf
    return pl.pallas_call(
        functools.partial(_ffn_kernel, sub=sub, norm_out=out_g is not None),
        grid=(s // tm, nf),
        in_specs=[
            pl.BlockSpec(memory_space=pl.ANY),
            pl.BlockSpec((1, d), lambda i, j: (0, 0)),
            pl.BlockSpec((1, d), lambda i, j: (0, 0)),
            pl.BlockSpec((d, tf), lambda i, j: (0, j)),
            pl.BlockSpec((d, tf), lambda i, j: (0, j)),
            pl.BlockSpec((V7X_SUBLANES, tf), lambda i, j: (0, j)),
            pl.BlockSpec((V7X_SUBLANES, tf), lambda i, j: (0, j)),
            pl.BlockSpec((tf, d), lambda i, j: (j, 0)),
        ],
        out_specs=pl.BlockSpec((tm, d), lambda i, j: (i, 0), pipeline_mode=pl.Buffered(1)),
        out_shape=jax.ShapeDtypeStruct((s, d), F32),
        scratch_shapes=[pltpu.VMEM((tm, d), BF16), pltpu.VMEM((2, NORM_ROWS, d), F32), pltpu.SemaphoreType.DMA((2,)),
                        pltpu.VMEM((nf, V7X_SUBLANES, tf), F32), pltpu.VMEM((nf, V7X_SUBLANES, tf), F32)],
        compiler_params=_params(("arbitrary", "arbitrary"), FFN_VMEM_LIMIT_BYTES),
        name="conv_ffn",
    )(x, norm_g.reshape(1, d), (norm_g if out_g is None else out_g).reshape(1, d), w_gate, w_up, tab_gate, tab_up, w_down)


def _gmlp_gate_kernel(z_ref, lng_ref, lnb_ref, ws_ref, bs_ref, o_ref):
    rows = z_ref.shape[0]
    d = o_ref.shape[1]
    gd = d // GMLP_GROUPS
    v = z_ref[:, d:].astype(F32)
    mu = jnp.mean(v, axis=-1, keepdims=True)
    vc = v - mu
    var = jnp.mean(vc * vc, axis=-1, keepdims=True)
    vln = ((vc * lax.rsqrt(var + LN_EPS)) * lng_ref[...] + lnb_ref[...]).astype(BF16)
    t_idx = lax.broadcasted_iota(jnp.int32, (GMLP_CHUNK, GMLP_CHUNK), 0)
    s_idx = lax.broadcasted_iota(jnp.int32, (GMLP_CHUNK, GMLP_CHUNK), 1)
    bs = bs_ref[...]
    for g in range(GMLP_GROUPS):
        w = jnp.where(s_idx <= t_idx, ws_ref[g], 0.0).astype(BF16)
        bias = bs[:, g:g + 1]
        for c in range(rows // GMLP_CHUNK):
            r0 = c * GMLP_CHUNK
            sv = jnp.dot(w, vln[r0:r0 + GMLP_CHUNK, g * gd:(g + 1) * gd], preferred_element_type=F32) + bias
            u = z_ref[r0:r0 + GMLP_CHUNK, g * gd:(g + 1) * gd].astype(F32)
            o_ref[r0:r0 + GMLP_CHUNK, g * gd:(g + 1) * gd] = (u * sv).astype(o_ref.dtype)


def _gmlp_gate(z, ln_g, ln_b, w_s, b_s, rows=STREAM_ROWS):
    s, d2 = z.shape
    d = d2 // 2
    return pl.pallas_call(
        _gmlp_gate_kernel,
        grid=(s // rows,),
        in_specs=[
            pl.BlockSpec((rows, d2), lambda i: (i, 0)),
            pl.BlockSpec((1, d), lambda i: (0, 0)),
            pl.BlockSpec((1, d), lambda i: (0, 0)),
            pl.BlockSpec(w_s.shape, lambda i: (0, 0, 0)),
            pl.BlockSpec((GMLP_CHUNK, GMLP_GROUPS), lambda i: (0, 0)),
        ],
        out_specs=pl.BlockSpec((rows, d), lambda i: (i, 0)),
        out_shape=jax.ShapeDtypeStruct((s, d), BF16),
        compiler_params=_params(("parallel",)),
        name="gmlp_gate",
    )(z, ln_g.reshape(1, d), ln_b.reshape(1, d), w_s, b_s.T)


def _gmlp_mixer(x, xn, w_in, ln_g, ln_b, w_s, b_s, w_out):
    z = _matmul(xn, w_in, out_dtype=BF16, epilogue="gelu", name="gmlp_in")
    y = _gmlp_gate(z, ln_g, ln_b, w_s, b_s)
    return _matmul(y, w_out, out_dtype=F32, epilogue="residual", extra=(x,), name="gmlp_out")


def _moba_select_kernel(q_ref, k_ref, o_ref, kmean_ref):
    r = pl.program_id(1)
    rows, _ = q_ref.shape
    blk = MOBA_BLOCK
    nb = k_ref.shape[0] // blk
    nbp = kmean_ref.shape[0]

    @pl.when(r == 0)
    def _():
        kmean_ref[...] = jnp.zeros(kmean_ref.shape, F32)

        def mean_body(n, carry):
            kb = k_ref[pl.ds(pl.multiple_of(n * blk, blk), blk), :]
            kmean_ref[pl.ds(n, 1), :] = jnp.mean(kb.astype(F32), axis=0, keepdims=True)
            return carry

        lax.fori_loop(0, nb, mean_body, 0)

    n_idx = lax.broadcasted_iota(jnp.int32, (nbp, rows), 0)
    own = (r * rows + lax.broadcasted_iota(jnp.int32, (nbp, rows), 1)) // blk
    gate = lax.dot_general(kmean_ref[...].astype(BF16), q_ref[...], (((1,), (1,)), ((), ())),
                           preferred_element_type=F32)
    valid = n_idx < own
    g = jnp.where(valid, gate, NEG_INF)
    sel = jnp.zeros((nbp, rows), jnp.bool_)
    for _ in range(MOBA_TOPK):
        m = jnp.max(g, axis=0, keepdims=True)
        idx = jnp.min(jnp.where(g == m, n_idx, nbp), axis=0, keepdims=True)
        hit = n_idx == idx
        sel = jnp.logical_or(sel, hit)
        g = jnp.where(hit, -jnp.inf, g)
    keep = jnp.logical_or(jnp.logical_and(sel, valid), n_idx == nb)
    o_ref[...] = jnp.where(keep, 0.0, NEG_INF).T.astype(o_ref.dtype)


def _moba_select(qkv, heads, rows=SELECT_ROWS):
    s, d3 = qkv.shape
    dh = d3 // (3 * heads)
    nb = s // MOBA_BLOCK
    nbp = -(-(nb + 1) // V7X_LANES) * V7X_LANES
    rows = min(rows, s)
    return pl.pallas_call(
        _moba_select_kernel,
        grid=(heads, s // rows),
        in_specs=[
            pl.BlockSpec((rows, dh), lambda h, r: (r, h)),
            pl.BlockSpec((s, dh), lambda h, r: (0, heads + h)),
        ],
        out_specs=pl.BlockSpec((rows, nbp), lambda h, r: (r, h)),
        out_shape=jax.ShapeDtypeStruct((s, heads * nbp), BF16),
        scratch_shapes=[pltpu.VMEM((nbp, dh), F32)],
        compiler_params=_params(("arbitrary", "arbitrary")),
        name="moba_select",
    )(qkv, qkv)


def _moba_kernel(q_ref, b_ref, k_ref, v_ref, o_ref, kaug_ref, vext_ref, s_ref, *, scale, group):
    i = pl.program_id(1)
    blk, dh = q_ref.shape
    nb = k_ref.shape[0] // blk
    nbp = b_ref.shape[1]
    gk = group * blk
    c_exp = scale * LOG2E
    lane = lax.broadcasted_iota(jnp.int32, (blk, nbp), 1)

    @pl.when(i == 0)
    def _():
        def head_setup(n, carry):
            rows = pl.ds(pl.multiple_of(n * blk, blk), blk)
            kaug_ref[rows, :] = jnp.concatenate([k_ref[rows, :], (lane == n).astype(BF16)], axis=1)
            vext_ref[rows, :] = jnp.concatenate([v_ref[rows, :], (lane == 0).astype(BF16)], axis=1)
            return carry

        lax.fori_loop(0, nb, head_setup, 0)

    q_aug = jnp.concatenate([q_ref[...], b_ref[...]], axis=1)
    nt = (((1,), (1,)), ((), ()))

    own_row = pl.multiple_of(i * blk, blk)
    rest = (group - 1) * blk
    half = (group // 2) * blk
    k_own = jnp.concatenate([k_ref[pl.ds(own_row, blk), :], (lane == nb).astype(BF16)], axis=1)
    s_own = lax.dot_general(q_aug, k_own, nt, preferred_element_type=F32)
    row = lax.broadcasted_iota(jnp.int32, (blk, blk), 0)
    col = lax.broadcasted_iota(jnp.int32, (blk, blk), 1)
    s_ref[0, :, 0:blk] = jnp.where(col <= row, s_own, NEG_INF)
    s_ref[0, :, blk:half] = lax.dot_general(q_aug, kaug_ref[0:half - blk, :], nt, preferred_element_type=F32)
    s_ref[0, :, half:gk] = lax.dot_general(q_aug, kaug_ref[half - blk:rest, :], nt, preferred_element_type=F32)

    last_step = nb // group - 1

    def step(j, slot, m_prev, acc, look_ahead=True):
        if look_ahead:
            ahead = jnp.minimum(j + 1, last_step)
            rows_ahead = pl.ds(pl.multiple_of((ahead * group - 1) * blk, blk), gk)
            s_ref[1 - slot] = lax.dot_general(q_aug, kaug_ref[rows_ahead, :], nt, preferred_element_type=F32)
        sc = s_ref[slot]
        m_new = jnp.maximum(m_prev, jnp.max(sc, axis=1, keepdims=True))
        alpha = jnp.exp2((m_prev - m_new) * c_exp)
        p = jnp.exp2((sc - m_new) * c_exp).astype(BF16)
        head_row = pl.multiple_of(jnp.where(j == 0, own_row, (j * group - 1) * blk), blk)
        tail_row = pl.multiple_of(jnp.where(j == 0, 0, j * gk), blk)
        pv = (jnp.dot(p[:, 0:blk], vext_ref[pl.ds(head_row, blk), :], preferred_element_type=F32)
              + jnp.dot(p[:, blk:gk], vext_ref[pl.ds(tail_row, rest), :], preferred_element_type=F32))
        return m_new, alpha * acc + pv

    def steps(first, count, carry, last_looks_ahead=True):
        for k in range(count):
            carry = step(first + k, k % 2, *carry, look_ahead=last_looks_ahead or k < count - 1)
        return carry

    n_steps = i // group + 1
    n_quads = n_steps // 4
    init = (jnp.full((blk, 1), -jnp.inf, F32), jnp.zeros((blk, 2 * dh), F32))
    carry = lax.fori_loop(0, n_quads, lambda t, c: steps(4 * t, 4, c), init)
    carry = lax.cond(n_steps % 4 >= 2, lambda: steps(4 * n_quads, 2, carry), lambda: carry)
    _, acc = lax.cond(n_steps % 2 == 1,
                      lambda: steps(n_steps - 1, 1, carry, last_looks_ahead=False),
                      lambda: carry)
    o_ref[...] = (acc[:, :dh] / acc[:, dh:dh + 1]).astype(o_ref.dtype)


def _moba_attention(qkv, bias, heads, group):
    s, d3 = qkv.shape
    d = d3 // 3
    dh = d // heads
    nb = s // MOBA_BLOCK
    nbp = bias.shape[1] // heads
    assert nb % (2 * group) == 0 and group % 2 == 0 and dh == V7X_LANES and nb < nbp
    return pl.pallas_call(
        functools.partial(_moba_kernel, scale=dh ** -0.5, group=group),
        grid=(heads, nb),
        in_specs=[
            pl.BlockSpec((MOBA_BLOCK, dh), lambda h, i: (i, h)),
            pl.BlockSpec((MOBA_BLOCK, nbp), lambda h, i: (i, h)),
            pl.BlockSpec((s, dh), lambda h, i: (0, heads + h)),
            pl.BlockSpec((s, dh), lambda h, i: (0, 2 * heads + h)),
        ],
        out_specs=pl.BlockSpec((MOBA_BLOCK, dh), lambda h, i: (i, h)),
        out_shape=jax.ShapeDtypeStruct((s, d), BF16),
        scratch_shapes=[pltpu.VMEM((s, dh + nbp), BF16), pltpu.VMEM((s, 2 * dh), BF16),
                        pltpu.VMEM((2, MOBA_BLOCK, group * MOBA_BLOCK), F32)],
        compiler_params=_params(("arbitrary", "arbitrary")),
        name="moba_attention",
    )(qkv, bias, qkv, qkv)


def _rope_tables(s, dh):
    rope_dim = dh // ROPE_FRACTION
    inv = ROPE_THETA ** (-jnp.arange(0, rope_dim, 2, dtype=F32) / rope_dim)
    ang = jnp.arange(s).astype(F32)[:, None] * inv[None, :]
    cos, sin = jnp.cos(ang), jnp.sin(ang)
    pad = dh - rope_dim
    cos_t = jnp.concatenate([cos, cos, jnp.ones((s, pad), F32)], axis=1)
    sin_t = jnp.concatenate([-sin, sin, jnp.zeros((s, pad), F32)], axis=1)
    return cos_t, sin_t


def _moba_mixer(x, xn, w_qkv, w_out, group=MOBA_GROUP):
    s, d = x.shape
    dh = d // MOBA_HEADS
    cos_t, sin_t = _rope_tables(s, dh)
    qkv = _matmul(xn, w_qkv, out_dtype=BF16, epilogue="rope", extra=(cos_t, sin_t),
                  rope_tiles=2 * d // PROJ_COLS, name="moba_qkv")
    att = _moba_attention(qkv, _moba_select(qkv, MOBA_HEADS), MOBA_HEADS, group)
    return _matmul(att, w_out, out_dtype=F32, epilogue="residual", extra=(x,), name="moba_out")


def _mlstm_kernel(q_ref, k_ref, v_ref, og_ref, gc_ref, gr_ref, b_ref, o_ref, c_ref, m_ref):
    ci = pl.program_id(1)
    L, dk = q_ref.shape
    dv = v_ref.shape[1]

    @pl.when(ci == 0)
    def _():
        c_ref[...] = jnp.zeros(c_ref.shape, F32)
        m_ref[...] = jnp.zeros(m_ref.shape, F32)

    b = b_ref[0]
    b_i, b_f = b[:, 0:1], b[:, 1:2]
    gc = gc_ref[0]
    gr = gr_ref[0]
    li_col = gc[:, 0:1] + b_i
    lf_col = jax.nn.log_sigmoid(gc[:, 1:2] + b_f)
    li_row = gr[0:1, :] + b_i
    lf_row = jax.nn.log_sigmoid(gr[1:2, :] + b_f)

    t_idx = lax.broadcasted_iota(jnp.int32, (L, L), 0)
    s_idx = lax.broadcasted_iota(jnp.int32, (L, L), 1)
    causal = s_idx <= t_idx
    b_col = jnp.sum(jnp.where(causal, lf_row, 0.0), axis=1, keepdims=True)
    b_row = jnp.sum(jnp.where(t_idx <= s_idx, lf_col, 0.0), axis=0, keepdims=True)
    b_last = b_row[:, L - 1:L]
    m_prev = m_ref[0:1, 0:1]

    dmat = jnp.where(causal, b_col - b_row + li_row, NEG_INF)
    inter = b_col + m_prev
    m_t = jnp.maximum(inter, jnp.max(dmat, axis=1, keepdims=True))
    q = q_ref[...]
    k = k_ref[...]
    qk = lax.dot_general(q, k, (((1,), (1,)), ((), ())), preferred_element_type=F32) * (dk ** -0.5)
    a = (jnp.exp(dmat - m_t) * qk).astype(BF16)
    s_inter = jnp.exp(inter - m_t)

    ones_col = (lax.broadcasted_iota(jnp.int32, (L, V7X_LANES), 1) == 0).astype(BF16)
    v_ext = jnp.concatenate([v_ref[...], ones_col], axis=1)
    c_prev = c_ref[...]
    qc = jnp.dot(q, c_prev.astype(BF16), preferred_element_type=F32) * (dk ** -0.5)
    num_ext = s_inter * qc + jnp.dot(a, v_ext, preferred_element_type=F32)
    den = num_ext[:, dv:dv + 1]
    h = num_ext[:, :dv] / jnp.maximum(jnp.abs(den), jnp.exp(-m_t))
    o_ref[...] = (jax.nn.sigmoid(og_ref[...].astype(F32)) * h).astype(o_ref.dtype)

    g_col = b_last - b_col + li_col
    m_new = jnp.maximum(b_last + m_prev, jnp.max(g_col, axis=0, keepdims=True))
    decay = jnp.exp(b_last + m_prev - m_new)
    wk = jnp.exp(g_col - m_new)
    kw = (k.astype(F32) * wk).astype(BF16)
    upd = lax.dot_general(kw, v_ext, (((0,), (0,)), ((), ())), preferred_element_type=F32)
    c_ref[...] = decay * c_prev + upd
    m_ref[...] = jnp.broadcast_to(m_new, m_ref.shape)


def _mlstm_scan(proj, gates_col, gates_row, bias, heads, chunk):
    s = proj.shape[0]
    dv = proj.shape[1] // (3 * heads)
    dk = dv // 2
    nc = s // chunk
    return pl.pallas_call(
        _mlstm_kernel,
        grid=(heads, nc),
        in_specs=[
            pl.BlockSpec((chunk, dk), lambda h, c: (c, h)),
            pl.BlockSpec((chunk, dk), lambda h, c: (c, heads + h)),
            pl.BlockSpec((chunk, dv), lambda h, c: (c, heads + h)),
            pl.BlockSpec((chunk, dv), lambda h, c: (c, 2 * heads + h)),
            pl.BlockSpec((1, chunk, 2), lambda h, c: (h, c, 0)),
            pl.BlockSpec((1, 2, chunk), lambda h, c: (h, 0, c)),
            pl.BlockSpec((1, 1, 2), lambda h, c: (h, 0, 0)),
        ],
        out_specs=pl.BlockSpec((chunk, dv), lambda h, c: (c, h)),
        out_shape=jax.ShapeDtypeStruct((s, heads * dv), BF16),
        scratch_shapes=[pltpu.VMEM((dk, dv + V7X_LANES), F32), pltpu.VMEM((V7X_SUBLANES, V7X_LANES), F32)],
        compiler_params=_params(("arbitrary", "arbitrary")),
        name="mlstm_scan",
    )(proj, proj, proj, proj, gates_col, gates_row, bias)


def _mlstm_mixer(x, xn, w_in, b_i, b_f, w_out, chunk=MLSTM_CHUNK):
    h = MLSTM_HEADS
    n_main = w_in.shape[1] - 2 * h
    w_gates = jnp.pad(w_in[:, n_main:], ((0, 0), (0, V7X_LANES - 2 * h)))
    proj = _matmul(xn, w_in, out_dtype=BF16, n=n_main, name="mlstm_in")
    gates = _matmul(xn, w_gates, tn=V7X_LANES, out_dtype=F32, name="mlstm_gates")
    pre = jnp.stack([gates[:, :h], gates[:, h:2 * h]], axis=0)
    gates_col = pre.transpose(2, 1, 0)
    gates_row = pre.transpose(2, 0, 1)
    bias = jnp.stack([b_i, b_f], axis=-1).reshape(h, 1, 2)
    hs = _mlstm_scan(proj, gates_col, gates_row, bias, h, chunk)
    return _matmul(hs, w_out, out_dtype=F32, epilogue="residual", extra=(x,), name="mlstm_out")


def _ffn_block(x, norm_g, w_up, conv_w, conv_b, w_down, out_g=None):
    f = w_down.shape[0]
    fpad = -f % FFN_TILE

    tab = jnp.concatenate([conv_w, conv_b[None, :], jnp.zeros((V7X_SUBLANES - CONV_W - 1, 2 * f), F32)], axis=0)
    tabs = [jnp.pad(tab[:, lo:lo + f], ((0, 0), (0, fpad))) for lo in (0, f)]
    return _conv_ffn(x, norm_g, out_g, _cast_pad(w_up, 1, 0, f, f + fpad), _cast_pad(w_up, 1, f, f, f + fpad), *tabs,
                     _cast_pad(w_down, 0, 0, f, f + fpad))


def kernel(x, l0_norm_mix, l0_a_w_in, l0_a_ln_g, l0_a_ln_b, l0_a_w_s, l0_a_b_s, l0_a_w_out, l0_norm_ffn, l0_ffn_w_up, l0_ffn_conv_w, l0_ffn_conv_b, l0_ffn_w_down, l1_norm_mix, l1_b_w_qkv, l1_b_w_out, l1_norm_ffn, l1_ffn_w_up, l1_ffn_conv_w, l1_ffn_conv_b, l1_ffn_w_down, l2_norm_mix, l2_c_w_in, l2_c_b_i, l2_c_b_f, l2_c_w_out, l2_norm_ffn, l2_ffn_w_up, l2_ffn_conv_w, l2_ffn_conv_b, l2_ffn_w_down, l3_norm_mix, l3_a_w_in, l3_a_ln_g, l3_a_ln_b, l3_a_w_s, l3_a_b_s, l3_a_w_out, l3_norm_ffn, l3_ffn_w_up, l3_ffn_conv_w, l3_ffn_conv_b, l3_ffn_w_down, final_norm):
    batch, s, d = x.shape
    outs = []
    for bidx in range(batch):
        h = x.reshape(s, d) if batch == 1 else x[bidx]
        hn = _rmsnorm(h, l0_norm_mix, BF16)
        h = _gmlp_mixer(h, hn, l0_a_w_in, l0_a_ln_g, l0_a_ln_b, l0_a_w_s, l0_a_b_s, l0_a_w_out)
        h = _ffn_block(h, l0_norm_ffn, l0_ffn_w_up, l0_ffn_conv_w, l0_ffn_conv_b, l0_ffn_w_down)
        hn = _rmsnorm(h, l1_norm_mix, BF16)
        h = _moba_mixer(h, hn, l1_b_w_qkv, l1_b_w_out)
        h = _ffn_block(h, l1_norm_ffn, l1_ffn_w_up, l1_ffn_conv_w, l1_ffn_conv_b, l1_ffn_w_down)
        hn = _rmsnorm(h, l2_norm_mix, BF16)
        h = _mlstm_mixer(h, hn, l2_c_w_in, l2_c_b_i, l2_c_b_f, l2_c_w_out)
        h = _ffn_block(h, l2_norm_ffn, l2_ffn_w_up, l2_ffn_conv_w, l2_ffn_conv_b, l2_ffn_w_down)
        hn = _rmsnorm(h, l3_norm_mix, BF16)
        h = _gmlp_mixer(h, hn, l3_a_w_in, l3_a_ln_g, l3_a_ln_b, l3_a_w_s, l3_a_b_s, l3_a_w_out)
        outs.append(_ffn_block(h, l3_norm_ffn, l3_ffn_w_up, l3_ffn_conv_w, l3_ffn_conv_b, l3_ffn_w_down,
                               out_g=final_norm))
    return outs[0].reshape(1, s, d) if batch == 1 else jnp.stack(outs, axis=0)
```

```python
import functools
import math

import jax
import jax.numpy as jnp
from jax import lax
from jax.experimental import pallas as pl
from jax.experimental.pallas import tpu as pltpu

F32 = jnp.float32
BF16 = jnp.bfloat16

V7X_LANES = 128
V7X_SUBLANES = 8
V7X_VMEM_BYTES = 64 * 1024 * 1024
VMEM_LIMIT_BYTES = V7X_VMEM_BYTES - 8 * 1024 * 1024
FFN_VMEM_LIMIT_BYTES = V7X_VMEM_BYTES - 2 * 1024 * 1024

GMLP_CHUNK = 128
GMLP_GROUPS = 8
MOBA_HEADS = 32
MOBA_BLOCK = 256
MOBA_TOPK = 3
ROPE_THETA = 500000.0
ROPE_FRACTION = 4
MLSTM_HEADS = 8
CONV_W = 3
NEG_INF = -1e30
RMS_EPS = 1e-6
LN_EPS = 1e-5
LOG2E = math.log2(math.e)
NORM_ROWS = 32

PROJ_ROWS, PROJ_COLS = 1024, 512
FFN_ROWS, FFN_TILE, FFN_PIECE = 1024, 512, 256
STREAM_ROWS = 512
SELECT_ROWS = 1024
MOBA_GROUP = 8
MOBA_TILE_BLOCKS = 2
MLSTM_CHUNK = 256
CAST_BLOCK = 256


def _params(semantics, vmem_limit_bytes=VMEM_LIMIT_BYTES):
    return pltpu.CompilerParams(dimension_semantics=semantics, vmem_limit_bytes=vmem_limit_bytes)


def _rmsnorm_kernel(x_ref, g_ref, o_ref):
    x = x_ref[...]
    y = x * lax.rsqrt(jnp.mean(x * x, axis=-1, keepdims=True) + RMS_EPS)
    o_ref[...] = (y * g_ref[...]).astype(o_ref.dtype)


def _rmsnorm(x, g, out_dtype, rows=STREAM_ROWS):
    s, d = x.shape
    return pl.pallas_call(
        _rmsnorm_kernel,
        grid=(s // rows,),
        in_specs=[pl.BlockSpec((rows, d), lambda i: (i, 0)), pl.BlockSpec((1, d), lambda i: (0, 0))],
        out_specs=pl.BlockSpec((rows, d), lambda i: (i, 0)),
        out_shape=jax.ShapeDtypeStruct((s, d), out_dtype),
        compiler_params=_params(("parallel",)),
        name="rmsnorm",
    )(x, g.reshape(1, d))


def _gelu(x):
    return 0.5 * x * (1.0 + lax.erf(x * math.sqrt(0.5)))


def _rope_heads(acc, cos, sin):
    rows, width = acc.shape
    lane = lax.broadcasted_iota(jnp.int32, (rows, V7X_LANES), 1)
    half = cos.shape[-1] // (2 * ROPE_FRACTION)
    outs = []
    for h in range(width // V7X_LANES):
        xh = acc[:, h * V7X_LANES:(h + 1) * V7X_LANES]
        partner = jnp.where(lane < half, pltpu.roll(xh, V7X_LANES - half, axis=1), pltpu.roll(xh, half, axis=1))
        outs.append(xh * cos + partner * sin)
    return jnp.concatenate(outs, axis=1) if len(outs) > 1 else outs[0]


def _mm_kernel(*refs, epilogue, rope_tiles):
    x_ref, w_ref = refs[0], refs[1]
    o_ref = refs[-1]
    acc = jnp.dot(x_ref[...], w_ref[...].astype(BF16), preferred_element_type=F32)
    if epilogue == "cast":
        o_ref[...] = acc.astype(o_ref.dtype)
    elif epilogue == "gelu":
        o_ref[...] = _gelu(acc).astype(o_ref.dtype)
    elif epilogue == "residual":
        o_ref[...] = (refs[2][...] + acc).astype(o_ref.dtype)
    elif epilogue == "rope":
        cos_ref, sin_ref = refs[2], refs[3]
        j = pl.program_id(1)

        @pl.when(j < rope_tiles)
        def _():
            o_ref[...] = _rope_heads(acc, cos_ref[...], sin_ref[...]).astype(o_ref.dtype)

        @pl.when(j >= rope_tiles)
        def _():
            o_ref[...] = acc.astype(o_ref.dtype)
    else:
        raise ValueError(epilogue)


def _matmul(x, w, *, out_dtype, tm=PROJ_ROWS, tn=PROJ_COLS, epilogue="cast", extra=(), rope_tiles=0, n=None, name="matmul"):
    m, k = x.shape
    n = n or w.shape[1]
    tm, tn = min(tm, m), min(tn, n)
    in_specs = [pl.BlockSpec((tm, k), lambda i, j: (i, 0)), pl.BlockSpec((k, tn), lambda i, j: (0, j))]
    if epilogue == "residual":
        in_specs.append(pl.BlockSpec((tm, tn), lambda i, j: (i, j)))
    elif epilogue == "rope":
        in_specs += [pl.BlockSpec((tm, V7X_LANES), lambda i, j: (i, 0))] * 2
    return pl.pallas_call(
        functools.partial(_mm_kernel, epilogue=epilogue, rope_tiles=rope_tiles),
        grid=(m // tm, n // tn),
        in_specs=in_specs,
        out_specs=pl.BlockSpec((tm, tn), lambda i, j: (i, j)),
        out_shape=jax.ShapeDtypeStruct((m, n), out_dtype),
        compiler_params=_params(("parallel", "parallel")),
        name=name,
    )(x, w, *extra)


def _cast_pad_kernel(w_ref, o_ref, *, valid_blocks):
    j = pl.program_id(0)

    @pl.when(j < valid_blocks)
    def _():
        o_ref[...] = w_ref[...].astype(o_ref.dtype)

    @pl.when(j >= valid_blocks)
    def _():
        o_ref[...] = jnp.zeros(o_ref.shape, o_ref.dtype)


def _cast_pad(w, axis, start, size, out_size, blk=CAST_BLOCK):
    assert start % blk == 0 and size % blk == 0 and out_size % blk == 0
    first, valid = start // blk, size // blk
    other = w.shape[1 - axis]
    if axis == 1:
        block = (other, blk)
        in_map = lambda j: (0, first + jnp.minimum(j, valid - 1))
        out_map = lambda j: (0, j)
        out_shape = (other, out_size)
    else:
        block = (blk, other)
        in_map = lambda j: (first + jnp.minimum(j, valid - 1), 0)
        out_map = lambda j: (j, 0)
        out_shape = (out_size, other)
    return pl.pallas_call(
        functools.partial(_cast_pad_kernel, valid_blocks=valid),
        grid=(out_size // blk,),
        in_specs=[pl.BlockSpec(block, in_map)],
        out_specs=pl.BlockSpec(block, out_map),
        out_shape=jax.ShapeDtypeStruct(out_shape, BF16),
        compiler_params=_params(("parallel",)),
        name="cast_pad",
    )(w)


def _ffn_kernel(x_hbm, g_ref, og_ref, wg_ref, wu_ref, cg_ref, cu_ref, wd_ref, o_ref, xn_ref, xbuf, xsem,
                carry_g, carry_u, *, sub, norm_out):
    i, j = pl.program_id(0), pl.program_id(1)
    tm = o_ref.shape[0]
    tf = wg_ref.shape[1]
    n_chunks = tm // NORM_ROWS

    def x_copy(c, slot):
        rows = pl.ds(pl.multiple_of(i * tm + c * NORM_ROWS, NORM_ROWS), NORM_ROWS)
        return pltpu.make_async_copy(x_hbm.at[rows, :], xbuf.at[slot], xsem.at[slot])

    @pl.when(j == 0)
    def _():
        x_copy(0, 0).start()

        def norm_rows(c, carry):
            slot = c % 2
            x_copy(c, slot).wait()

            @pl.when(c + 1 < n_chunks)
            def _():
                x_copy(c + 1, 1 - slot).start()

            rows = pl.ds(pl.multiple_of(c * NORM_ROWS, NORM_ROWS), NORM_ROWS)
            x = xbuf[slot]
            y = x * lax.rsqrt(jnp.mean(x * x, axis=-1, keepdims=True) + RMS_EPS)
            xn_ref[rows, :] = (y * g_ref[...]).astype(BF16)
            o_ref[rows, :] = x
            return carry

        lax.fori_loop(0, n_chunks, norm_rows, 0)

    @pl.when(i == 0)
    def _():
        carry_g[j] = jnp.zeros(carry_g.shape[1:], F32)
        carry_u[j] = jnp.zeros(carry_u.shape[1:], F32)

    def conv(a, carry, c_ref, lanes):
        prev = carry[j, :, lanes]
        carry[j, :, lanes] = a[tm - V7X_SUBLANES:, :]
        ext = jnp.concatenate([prev, a], axis=0)
        a1 = ext[V7X_SUBLANES - 1:V7X_SUBLANES - 1 + tm, :]
        a2 = ext[V7X_SUBLANES - 2:V7X_SUBLANES - 2 + tm, :]
        c = c_ref[:, lanes]
        return c[3:4, :] + c[0:1, :] * a2 + c[1:2, :] * a1 + c[2:3, :] * a

    xn = xn_ref[...]
    pieces = [slice(k * sub, (k + 1) * sub) for k in range(tf // sub)]
    ups = [(jnp.dot(xn, wg_ref[:, p], preferred_element_type=F32), jnp.dot(xn, wu_ref[:, p], preferred_element_type=F32))
           for p in pieces]
    for p, (ag, au) in zip(pieces, ups):
        act = (_gelu(conv(ag, carry_g, cg_ref, p)) * conv(au, carry_u, cu_ref, p)).astype(BF16)
        o_ref[...] += jnp.dot(act, wd_ref[p, :], preferred_element_type=F32)

    if norm_out:
        @pl.when(j == pl.num_programs(1) - 1)
        def _():
            def norm_rows(c, carry):
                rows = pl.ds(pl.multiple_of(c * NORM_ROWS, NORM_ROWS), NORM_ROWS)
                y = o_ref[rows, :]
                y = y * lax.rsqrt(jnp.mean(y * y, axis=-1, keepdims=True) + RMS_EPS)
                o_ref[rows, :] = y * og_ref[...]
                return carry

            lax.fori_loop(0, tm // NORM_ROWS, norm_rows, 0)


def _conv_ffn(x, norm_g, out_g, w_gate, w_up, tab_gate, tab_up, w_down, *, tm=FFN_ROWS, tf=FFN_TILE, sub=FFN_PIECE):
    s, d = x.shape
    tm = min(tm, s)
    nf = w_down.shape[0] // tf
    return pl.pallas_call(
        functools.partial(_ffn_kernel, sub=sub, norm_out=out_g is not None),
        grid=(s // tm, nf),
        in_specs=[
            pl.BlockSpec(memory_space=pl.ANY),
            pl.BlockSpec((1, d), lambda i, j: (0, 0)),
            pl.BlockSpec((1, d), lambda i, j: (0, 0)),
            pl.BlockSpec((d, tf), lambda i, j: (0, j)),
            pl.BlockSpec((d, tf), lambda i, j: (0, j)),
            pl.BlockSpec((V7X_SUBLANES, tf), lambda i, j: (0, j)),
            pl.BlockSpec((V7X_SUBLANES, tf), lambda i, j: (0, j)),
            pl.BlockSpec((tf, d), lambda i, j: (j, 0)),
        ],
        out_specs=pl.BlockSpec((tm, d), lambda i, j: (i, 0), pipeline_mode=pl.Buffered(1)),
        out_shape=jax.ShapeDtypeStruct((s, d), F32),
        scratch_shapes=[pltpu.VMEM((tm, d), BF16), pltpu.VMEM((2, NORM_ROWS, d), F32), pltpu.SemaphoreType.DMA((2,)),
                        pltpu.VMEM((nf, V7X_SUBLANES, tf), F32), pltpu.VMEM((nf, V7X_SUBLANES, tf), F32)],
        compiler_params=_params(("arbitrary", "arbitrary"), FFN_VMEM_LIMIT_BYTES),
        name="conv_ffn",
    )(x, norm_g.reshape(1, d), (norm_g if out_g is None else out_g).reshape(1, d), w_gate, w_up, tab_gate, tab_up, w_down)


def _gmlp_gate_kernel(z_ref, lng_ref, lnb_ref, ws_ref, bs_ref, o_ref):
    rows = z_ref.shape[0]
    d = o_ref.shape[1]
    gd = d // GMLP_GROUPS
    v = z_ref[:, d:].astype(F32)
    mu = jnp.mean(v, axis=-1, keepdims=True)
    vc = v - mu
    var = jnp.mean(vc * vc, axis=-1, keepdims=True)
    vln = ((vc * lax.rsqrt(var + LN_EPS)) * lng_ref[...] + lnb_ref[...]).astype(BF16)
    t_idx = lax.broadcasted_iota(jnp.int32, (GMLP_CHUNK, GMLP_CHUNK), 0)
    s_idx = lax.broadcasted_iota(jnp.int32, (GMLP_CHUNK, GMLP_CHUNK), 1)
    bs = bs_ref[...]
    for g in range(GMLP_GROUPS):
        w = jnp.where(s_idx <= t_idx, ws_ref[g], 0.0).astype(BF16)
        bias = bs[:, g:g + 1]
        for c in range(rows // GMLP_CHUNK):
            r0 = c * GMLP_CHUNK
            sv = jnp.dot(w, vln[r0:r0 + GMLP_CHUNK, g * gd:(g + 1) * gd], preferred_element_type=F32) + bias
            u = z_ref[r0:r0 + GMLP_CHUNK, g * gd:(g + 1) * gd].astype(F32)
            o_ref[r0:r0 + GMLP_CHUNK, g * gd:(g + 1) * gd] = (u * sv).astype(o_ref.dtype)


def _gmlp_gate(z, ln_g, ln_b, w_s, b_s, rows=STREAM_ROWS):
    s, d2 = z.shape
    d = d2 // 2
    return pl.pallas_call(
        _gmlp_gate_kernel,
        grid=(s // rows,),
        in_specs=[
            pl.BlockSpec((rows, d2), lambda i: (i, 0)),
            pl.BlockSpec((1, d), lambda i: (0, 0)),
            pl.BlockSpec((1, d), lambda i: (0, 0)),
            pl.BlockSpec(w_s.shape, lambda i: (0, 0, 0)),
            pl.BlockSpec((GMLP_CHUNK, GMLP_GROUPS), lambda i: (0, 0)),
        ],
        out_specs=pl.BlockSpec((rows, d), lambda i: (i, 0)),
        out_shape=jax.ShapeDtypeStruct((s, d), BF16),
        compiler_params=_params(("parallel",)),
        name="gmlp_gate",
    )(z, ln_g.reshape(1, d), ln_b.reshape(1, d), w_s, b_s.T)


def _gmlp_mixer(x, xn, w_in, ln_g, ln_b, w_s, b_s, w_out):
    z = _matmul(xn, w_in, out_dtype=BF16, epilogue="gelu", name="gmlp_in")
    y = _gmlp_gate(z, ln_g, ln_b, w_s, b_s)
    return _matmul(y, w_out, out_dtype=F32, epilogue="residual", extra=(x,), name="gmlp_out")


def _moba_select_kernel(q_ref, k_ref, o_ref, kmean_ref):
    r = pl.program_id(1)
    rows, _ = q_ref.shape
    blk = MOBA_BLOCK
    nb = k_ref.shape[0] // blk
    nbp = kmean_ref.shape[0]

    @pl.when(r == 0)
    def _():
        kmean_ref[...] = jnp.zeros(kmean_ref.shape, F32)

        def mean_body(n, carry):
            kb = k_ref[pl.ds(pl.multiple_of(n * blk, blk), blk), :]
            kmean_ref[pl.ds(n, 1), :] = jnp.mean(kb.astype(F32), axis=0, keepdims=True)
            return carry

        lax.fori_loop(0, nb, mean_body, 0)

    n_idx = lax.broadcasted_iota(jnp.int32, (nbp, rows), 0)
    own = (r * rows + lax.broadcasted_iota(jnp.int32, (nbp, rows), 1)) // blk
    gate = lax.dot_general(kmean_ref[...].astype(BF16), q_ref[...], (((1,), (1,)), ((), ())),
                           preferred_element_type=F32)
    valid = n_idx < own
    g = jnp.where(valid, gate, NEG_INF)
    sel = jnp.zeros((nbp, rows), jnp.bool_)
    for _ in range(MOBA_TOPK):
        m = jnp.max(g, axis=0, keepdims=True)
        idx = jnp.min(jnp.where(g == m, n_idx, nbp), axis=0, keepdims=True)
        hit = n_idx == idx
        sel = jnp.logical_or(sel, hit)
        g = jnp.where(hit, -jnp.inf, g)
    tile_first = (own // MOBA_TILE_BLOCKS) * MOBA_TILE_BLOCKS
    keep = jnp.logical_and(sel, n_idx < tile_first)
    for c in range(MOBA_TILE_BLOCKS):
        chosen = jnp.max(jnp.where(jnp.logical_and(sel, n_idx == tile_first + c), 1.0, 0.0), axis=0, keepdims=True) > 0.0
        open_c = jnp.logical_or(own == tile_first + c, jnp.logical_and(own > tile_first + c, chosen))
        keep = jnp.logical_or(keep, jnp.logical_and(n_idx == nb + c, open_c))
    o_ref[...] = jnp.where(keep, 0.0, NEG_INF).T.astype(o_ref.dtype)


def _moba_select(qkv, heads, rows=SELECT_ROWS):
    s, d3 = qkv.shape
    dh = d3 // (3 * heads)
    nb = s // MOBA_BLOCK
    nbp = -(-(nb + MOBA_TILE_BLOCKS) // V7X_LANES) * V7X_LANES
    rows = min(rows, s)
    return pl.pallas_call(
        _moba_select_kernel,
        grid=(heads, s // rows),
        in_specs=[
            pl.BlockSpec((rows, dh), lambda h, r: (r, h)),
            pl.BlockSpec((s, dh), lambda h, r: (0, heads + h)),
        ],
        out_specs=pl.BlockSpec((rows, nbp), lambda h, r: (r, h)),
        out_shape=jax.ShapeDtypeStruct((s, heads * nbp), BF16),
        scratch_shapes=[pltpu.VMEM((nbp, dh), F32)],
        compiler_params=_params(("arbitrary", "arbitrary")),
        name="moba_select",
    )(qkv, qkv)


def _moba_kernel(q_ref, b_ref, k_ref, v_ref, o_ref, kaug_ref, vext_ref, s_ref, *, scale, group):
    i = pl.program_id(1)
    tq, dh = q_ref.shape
    blk = MOBA_BLOCK
    tile_blocks = tq // blk
    nb = k_ref.shape[0] // blk
    nbp = b_ref.shape[1]
    gk = group * blk
    c_exp = scale * LOG2E
    lane = lax.broadcasted_iota(jnp.int32, (blk, nbp), 1)

    @pl.when(i == 0)
    def _():
        def head_setup(n, carry):
            rows = pl.ds(pl.multiple_of(n * blk, blk), blk)
            kaug_ref[rows, :] = jnp.concatenate([k_ref[rows, :], (lane == n).astype(BF16)], axis=1)
            vext_ref[rows, :] = jnp.concatenate([v_ref[rows, :], (lane == 0).astype(BF16)], axis=1)
            return carry

        lax.fori_loop(0, nb, head_setup, 0)

    q_aug = jnp.concatenate([q_ref[...], b_ref[...]], axis=1)
    nt = (((1,), (1,)), ((), ()))

    own_row = pl.multiple_of(i * tq, tq)
    rest = gk - tq
    mid = tq + (rest // blk // 2) * blk
    own_code = (lax.broadcasted_iota(jnp.int32, (tq, nbp), 1)
                == nb + lax.broadcasted_iota(jnp.int32, (tq, nbp), 0) // blk).astype(BF16)
    k_own = jnp.concatenate([k_ref[pl.ds(own_row, tq), :], own_code], axis=1)
    s_own = lax.dot_general(q_aug, k_own, nt, preferred_element_type=F32)
    row = lax.broadcasted_iota(jnp.int32, (tq, tq), 0)
    col = lax.broadcasted_iota(jnp.int32, (tq, tq), 1)
    s_ref[0, :, 0:tq] = jnp.where(col <= row, s_own, NEG_INF)
    s_ref[0, :, tq:mid] = lax.dot_general(q_aug, kaug_ref[0:mid - tq, :], nt, preferred_element_type=F32)
    s_ref[0, :, mid:gk] = lax.dot_general(q_aug, kaug_ref[mid - tq:rest, :], nt, preferred_element_type=F32)

    last_step = nb // group - 1

    def step(j, slot, m_prev, acc, look_ahead=True):
        if look_ahead:
            ahead = jnp.minimum(j + 1, last_step)
            rows_ahead = pl.ds(pl.multiple_of((ahead * group - tile_blocks) * blk, blk), gk)
            s_ref[1 - slot] = lax.dot_general(q_aug, kaug_ref[rows_ahead, :], nt, preferred_element_type=F32)
        sc = s_ref[slot]
        m_new = jnp.maximum(m_prev, jnp.max(sc, axis=1, keepdims=True))
        alpha = jnp.exp2((m_prev - m_new) * c_exp)
        p = jnp.exp2((sc - m_new) * c_exp).astype(BF16)
        head_row = pl.multiple_of(jnp.where(j == 0, own_row, (j * group - tile_blocks) * blk), blk)
        tail_row = pl.multiple_of(jnp.where(j == 0, 0, j * gk), blk)
        pv = (jnp.dot(p[:, 0:tq], vext_ref[pl.ds(head_row, tq), :], preferred_element_type=F32)
              + jnp.dot(p[:, tq:gk], vext_ref[pl.ds(tail_row, rest), :], preferred_element_type=F32))
        return m_new, alpha * acc + pv

    def steps(first, count, carry, last_looks_ahead=True):
        for k in range(count):
            carry = step(first + k, k % 2, *carry, look_ahead=last_looks_ahead or k < count - 1)
        return carry

    n_steps = (tile_blocks * i + tile_blocks - 1) // group + 1
    n_quads = n_steps // 4
    init = (jnp.full((tq, 1), -jnp.inf, F32), jnp.zeros((tq, 2 * dh), F32))
    carry = lax.fori_loop(0, n_quads, lambda t, c: steps(4 * t, 4, c), init)
    carry = lax.cond(n_steps % 4 >= 2, lambda: steps(4 * n_quads, 2, carry), lambda: carry)
    _, acc = lax.cond(n_steps % 2 == 1,
                      lambda: steps(n_steps - 1, 1, carry, last_looks_ahead=False),
                      lambda: carry)
    o_ref[...] = (acc[:, :dh] / acc[:, dh:dh + 1]).astype(o_ref.dtype)


def _moba_attention(qkv, bias, heads, group):
    s, d3 = qkv.shape
    d = d3 // 3
    dh = d // heads
    nb = s // MOBA_BLOCK
    nbp = bias.shape[1] // heads
    tq = MOBA_TILE_BLOCKS * MOBA_BLOCK
    assert nb % (2 * group) == 0 and (group - MOBA_TILE_BLOCKS) % 2 == 0 and group > MOBA_TILE_BLOCKS
    assert dh == V7X_LANES and nb + MOBA_TILE_BLOCKS <= nbp
    return pl.pallas_call(
        functools.partial(_moba_kernel, scale=dh ** -0.5, group=group),
        grid=(heads, s // tq),
        in_specs=[
            pl.BlockSpec((tq, dh), lambda h, i: (i, h)),
            pl.BlockSpec((tq, nbp), lambda h, i: (i, h)),
            pl.BlockSpec((s, dh), lambda h, i: (0, heads + h)),
            pl.BlockSpec((s, dh), lambda h, i: (0, 2 * heads + h)),
        ],
        out_specs=pl.BlockSpec((tq, dh), lambda h, i: (i, h)),
        out_shape=jax.ShapeDtypeStruct((s, d), BF16),
        scratch_shapes=[pltpu.VMEM((s, dh + nbp), BF16), pltpu.VMEM((s, 2 * dh), BF16),
                        pltpu.VMEM((2, tq, group * MOBA_BLOCK), F32)],
        compiler_params=_params(("arbitrary", "arbitrary")),
        name="moba_attention",
    )(qkv, bias, qkv, qkv)


def _rope_tables(s, dh):
    rope_dim = dh // ROPE_FRACTION
    inv = ROPE_THETA ** (-jnp.arange(0, rope_dim, 2, dtype=F32) / rope_dim)
    ang = jnp.arange(s).astype(F32)[:, None] * inv[None, :]
    cos, sin = jnp.cos(ang), jnp.sin(ang)
    pad = dh - rope_dim
    cos_t = jnp.concatenate([cos, cos, jnp.ones((s, pad), F32)], axis=1)
    sin_t = jnp.concatenate([-sin, sin, jnp.zeros((s, pad), F32)], axis=1)
    return cos_t, sin_t


def _moba_mixer(x, xn, w_qkv, w_out, group=MOBA_GROUP):
    s, d = x.shape
    dh = d // MOBA_HEADS
    cos_t, sin_t = _rope_tables(s, dh)
    qkv = _matmul(xn, w_qkv, out_dtype=BF16, epilogue="rope", extra=(cos_t, sin_t),
                  rope_tiles=2 * d // PROJ_COLS, name="moba_qkv")
    att = _moba_attention(qkv, _moba_select(qkv, MOBA_HEADS), MOBA_HEADS, group)
    return _matmul(att, w_out, out_dtype=F32, epilogue="residual", extra=(x,), name="moba_out")


def _mlstm_kernel(q_ref, k_ref, v_ref, og_ref, gc_ref, gr_ref, b_ref, o_ref, c_ref, m_ref):
    ci = pl.program_id(1)
    L, dk = q_ref.shape
    dv = v_ref.shape[1]

    @pl.when(ci == 0)
    def _():
        c_ref[...] = jnp.zeros(c_ref.shape, F32)
        m_ref[...] = jnp.zeros(m_ref.shape, F32)

    b = b_ref[0]
    b_i, b_f = b[:, 0:1], b[:, 1:2]
    gc = gc_ref[0]
    gr = gr_ref[0]
    li_col = gc[:, 0:1] + b_i
    lf_col = jax.nn.log_sigmoid(gc[:, 1:2] + b_f)
    li_row = gr[0:1, :] + b_i
    lf_row = jax.nn.log_sigmoid(gr[1:2, :] + b_f)

    t_idx = lax.broadcasted_iota(jnp.int32, (L, L), 0)
    s_idx = lax.broadcasted_iota(jnp.int32, (L, L), 1)
    causal = s_idx <= t_idx
    b_col = jnp.sum(jnp.where(causal, lf_row, 0.0), axis=1, keepdims=True)
    b_row = jnp.sum(jnp.where(t_idx <= s_idx, lf_col, 0.0), axis=0, keepdims=True)
    b_last = b_row[:, L - 1:L]
    m_prev = m_ref[0:1, 0:1]

    dmat = jnp.where(causal, b_col - b_row + li_row, NEG_INF)
    inter = b_col + m_prev
    m_t = jnp.maximum(inter, jnp.max(dmat, axis=1, keepdims=True))
    q = q_ref[...]
    k = k_ref[...]
    qk = lax.dot_general(q, k, (((1,), (1,)), ((), ())), preferred_element_type=F32) * (dk ** -0.5)
    a = (jnp.exp(dmat - m_t) * qk).astype(BF16)
    s_inter = jnp.exp(inter - m_t)

    ones_col = (lax.broadcasted_iota(jnp.int32, (L, V7X_LANES), 1) == 0).astype(BF16)
    v_ext = jnp.concatenate([v_ref[...], ones_col], axis=1)
    c_prev = c_ref[...]
    qc = jnp.dot(q, c_prev.astype(BF16), preferred_element_type=F32) * (dk ** -0.5)
    num_ext = s_inter * qc + jnp.dot(a, v_ext, preferred_element_type=F32)
    den = num_ext[:, dv:dv + 1]
    h = num_ext[:, :dv] / jnp.maximum(jnp.abs(den), jnp.exp(-m_t))
    o_ref[...] = (jax.nn.sigmoid(og_ref[...].astype(F32)) * h).astype(o_ref.dtype)

    g_col = b_last - b_col + li_col
    m_new = jnp.maximum(b_last + m_prev, jnp.max(g_col, axis=0, keepdims=True))
    decay = jnp.exp(b_last + m_prev - m_new)
    wk = jnp.exp(g_col - m_new)
    kw = (k.astype(F32) * wk).astype(BF16)
    upd = lax.dot_general(kw, v_ext, (((0,), (0,)), ((), ())), preferred_element_type=F32)
    c_ref[...] = decay * c_prev + upd
    m_ref[...] = jnp.broadcast_to(m_new, m_ref.shape)


def _mlstm_scan(proj, gates_col, gates_row, bias, heads, chunk):
    s = proj.shape[0]
    dv = proj.shape[1] // (3 * heads)
    dk = dv // 2
    nc = s // chunk
    return pl.pallas_call(
        _mlstm_kernel,
        grid=(heads, nc),
        in_specs=[
            pl.BlockSpec((chunk, dk), lambda h, c: (c, h)),
            pl.BlockSpec((chunk, dk), lambda h, c: (c, heads + h)),
            pl.BlockSpec((chunk, dv), lambda h, c: (c, heads + h)),
            pl.BlockSpec((chunk, dv), lambda h, c: (c, 2 * heads + h)),
            pl.BlockSpec((1, chunk, 2), lambda h, c: (h, c, 0)),
            pl.BlockSpec((1, 2, chunk), lambda h, c: (h, 0, c)),
            pl.BlockSpec((1, 1, 2), lambda h, c: (h, 0, 0)),
        ],
        out_specs=pl.BlockSpec((chunk, dv), lambda h, c: (c, h)),
        out_shape=jax.ShapeDtypeStruct((s, heads * dv), BF16),
        scratch_shapes=[pltpu.VMEM((dk, dv + V7X_LANES), F32), pltpu.VMEM((V7X_SUBLANES, V7X_LANES), F32)],
        compiler_params=_params(("arbitrary", "arbitrary")),
        name="mlstm_scan",
    )(proj, proj, proj, proj, gates_col, gates_row, bias)


def _mlstm_mixer(x, xn, w_in, b_i, b_f, w_out, chunk=MLSTM_CHUNK):
    h = MLSTM_HEADS
    n_main = w_in.shape[1] - 2 * h
    w_gates = jnp.pad(w_in[:, n_main:], ((0, 0), (0, V7X_LANES - 2 * h)))
    proj = _matmul(xn, w_in, out_dtype=BF16, n=n_main, name="mlstm_in")
    gates = _matmul(xn, w_gates, tn=V7X_LANES, out_dtype=F32, name="mlstm_gates")
    pre = jnp.stack([gates[:, :h], gates[:, h:2 * h]], axis=0)
    gates_col = pre.transpose(2, 1, 0)
    gates_row = pre.transpose(2, 0, 1)
    bias = jnp.stack([b_i, b_f], axis=-1).reshape(h, 1, 2)
    hs = _mlstm_scan(proj, gates_col, gates_row, bias, h, chunk)
    return _matmul(hs, w_out, out_dtype=F32, epilogue="residual", extra=(x,), name="mlstm_out")


def _ffn_block(x, norm_g, w_up, conv_w, conv_b, w_down, out_g=None):
    f = w_down.shape[0]
    fpad = -f % FFN_TILE

    tab = jnp.concatenate([conv_w, conv_b[None, :], jnp.zeros((V7X_SUBLANES - CONV_W - 1, 2 * f), F32)], axis=0)
    tabs = [jnp.pad(tab[:, lo:lo + f], ((0, 0), (0, fpad))) for lo in (0, f)]
    return _conv_ffn(x, norm_g, out_g, _cast_pad(w_up, 1, 0, f, f + fpad), _cast_pad(w_up, 1, f, f, f + fpad), *tabs,
                     _cast_pad(w_down, 0, 0, f, f + fpad))


def kernel(x, l0_norm_mix, l0_a_w_in, l0_a_ln_g, l0_a_ln_b, l0_a_w_s, l0_a_b_s, l0_a_w_out, l0_norm_ffn, l0_ffn_w_up, l0_ffn_conv_w, l0_ffn_conv_b, l0_ffn_w_down, l1_norm_mix, l1_b_w_qkv, l1_b_w_out, l1_norm_ffn, l1_ffn_w_up, l1_ffn_conv_w, l1_ffn_conv_b, l1_ffn_w_down, l2_norm_mix, l2_c_w_in, l2_c_b_i, l2_c_b_f, l2_c_w_out, l2_norm_ffn, l2_ffn_w_up, l2_ffn_conv_w, l2_ffn_conv_b, l2_ffn_w_down, l3_norm_mix, l3_a_w_in, l3_a_ln_g, l3_a_ln_b, l3_a_w_s, l3_a_b_s, l3_a_w_out, l3_norm_ffn, l3_ffn_w_up, l3_ffn_conv_w, l3_ffn_conv_b, l3_ffn_w_down, final_norm):
    batch, s, d = x.shape
    outs = []
    for bidx in range(batch):
        h = x.reshape(s, d) if batch == 1 else x[bidx]
        hn = _rmsnorm(h, l0_norm_mix, BF16)
        h = _gmlp_mixer(h, hn, l0_a_w_in, l0_a_ln_g, l0_a_ln_b, l0_a_w_s, l0_a_b_s, l0_a_w_out)
        h = _ffn_block(h, l0_norm_ffn, l0_ffn_w_up, l0_ffn_conv_w, l0_ffn_conv_b, l0_ffn_w_down)
        hn = _rmsnorm(h, l1_norm_mix, BF16)
        h = _moba_mixer(h, hn, l1_b_w_qkv, l1_b_w_out)
        h = _ffn_block(h, l1_norm_ffn, l1_ffn_w_up, l1_ffn_conv_w, l1_ffn_conv_b, l1_ffn_w_down)
        hn = _rmsnorm(h, l2_norm_mix, BF16)
        h = _mlstm_mixer(h, hn, l2_c_w_in, l2_c_b_i, l2_c_b_f, l2_c_w_out)
        h = _ffn_block(h, l2_norm_ffn, l2_ffn_w_up, l2_ffn_conv_w, l2_ffn_conv_b, l2_ffn_w_down)
        hn = _rmsnorm(h, l3_norm_mix, BF16)
        h = _gmlp_mixer(h, hn, l3_a_w_in, l3_a_ln_g, l3_a_ln_b, l3_a_w_s, l3_a_b_s, l3_a_w_out)
        outs.append(_ffn_block(h, l3_norm_ffn, l3_ffn_w_up, l3_ffn_conv_w, l3_ffn_conv_b, l3_ffn_w_down,
                               out_g=final_norm))
    return outs[0].reshape(1, s, d) if batch == 1 else jnp.stack(outs, axis=0)
```
